```python
import math
import jax, jax.numpy as jnp
from jax import lax
import numpy as np


D_MODEL = 1024
BATCH = 1
SEQ = 16384
DEPTH = 4

CHUNK = 64
N_MIXERS = 3
N_S5_LAYERS = (DEPTH + 2) // 3
N_GLA_LAYERS = (DEPTH + 1) // 3
N_DIFF_LAYERS = DEPTH // 3
NORM_EPS = 1e-6
S5_GROUP = 16
S5_GROUPS = D_MODEL // S5_GROUP
S5_STATE = 64
DT_MIN = 1e-3
DT_MAX = 1e-1
GLA_HEADS = 4
GLA_DK = D_MODEL // 2
GLA_DV = D_MODEL
GLA_DKH = GLA_DK // GLA_HEADS
GLA_DVH = GLA_DV // GLA_HEADS
GLA_GATE_RANK = 16
GLA_TEMP = 16.0
GLA_IN = 2 * GLA_DK + 2 * GLA_DV + GLA_GATE_RANK
DIFF_HD = 64
DIFF_HEADS = D_MODEL // (2 * DIFF_HD)
ROT_DIMS = DIFF_HD // 4
ROPE_THETA = 500000.0
Q_BLOCK = 128
NEG_INF = -1e30
FFN_HIDDEN = -(-8 * D_MODEL // (3 * 256)) * 256

kernel_name = 'hybrid_s5_gla_diffattn_trunk'


def rmsnorm(x, g):
    xf = x.astype(jnp.float32)
    y = xf * lax.rsqrt(jnp.mean(xf * xf, axis=-1, keepdims=True) + NORM_EPS)
    return (y * g.astype(jnp.float32)).astype(x.dtype)


def swiglu_ffn(h, w_gate_up, w_down):
    g, u = jnp.split(h @ w_gate_up, 2, axis=-1)
    return (jax.nn.silu(g) * u) @ w_down


def _cdiag_combine(e1, e2):
    a1r, a1i, b1r, b1i = e1
    a2r, a2i, b2r, b2i = e2
    ar = a2r * a1r - a2i * a1i
    ai = a2r * a1i + a2i * a1r
    br = a2r * b1r - a2i * b1i + b2r
    bi = a2r * b1i + a2i * b1r + b2i
    return (ar, ai, br, bi)


def s5_mixer(h, lam_re, lam_im, log_dt, b_re, b_im, c_re, c_im, d_skip, w_glu):
    bsz, seq, _ = h.shape
    n_chunks = seq // CHUNK
    f32 = jnp.float32
    lr = lam_re.astype(f32)
    li = lam_im.astype(f32)
    dt = jnp.exp(log_dt.astype(f32))[:, None]
    ab_mag = jnp.exp(lr * dt)
    ab_ang = li * dt
    ab_re = ab_mag * jnp.cos(ab_ang)
    ab_im = ab_mag * jnp.sin(ab_ang)
    den = lr * lr + li * li
    f_re = ((ab_re - 1.0) * lr + ab_im * li) / den
    f_im = (ab_im * lr - (ab_re - 1.0) * li) / den
    br = b_re.astype(f32)
    bi = b_im.astype(f32)
    bb_re = f_re[..., None] * br - f_im[..., None] * bi
    bb_im = f_re[..., None] * bi + f_im[..., None] * br
    j = jnp.arange(1, CHUNK + 1, dtype=f32)[:, None, None]
    p_mag = jnp.exp(lr * dt * j)
    p_ang = li * dt * j
    p_re = p_mag * jnp.cos(p_ang)
    p_im = p_mag * jnp.sin(p_ang)
    cr = c_re.astype(f32)
    ci = c_im.astype(f32)
    u = h.astype(f32).reshape(bsz, n_chunks, CHUNK, S5_GROUPS, S5_GROUP)
    u = jnp.moveaxis(u, 1, 0)

    def chunk_step(carry, u_c):
        s_re0, s_im0 = carry
        bu_re = jnp.einsum('bcgh,gph->bcgp', u_c, bb_re)
        bu_im = jnp.einsum('bcgh,gph->bcgp', u_c, bb_im)
        a_re = jnp.broadcast_to(ab_re, bu_re.shape)
        a_im = jnp.broadcast_to(ab_im, bu_re.shape)
        _, _, s_re, s_im = lax.associative_scan(_cdiag_combine, (a_re, a_im, bu_re, bu_im), axis=1)
        s_re = s_re + p_re * s_re0[:, None] - p_im * s_im0[:, None]
        s_im = s_im + p_re * s_im0[:, None] + p_im * s_re0[:, None]
        y = jnp.einsum('bcgp,ghp->bcgh', s_re, cr) - jnp.einsum('bcgp,ghp->bcgh', s_im, ci)
        return (s_re[:, -1], s_im[:, -1]), y

    zeros = jnp.zeros((bsz, S5_GROUPS, S5_STATE), f32)
    _, y = lax.scan(chunk_step, (zeros, zeros), u)
    y = jnp.moveaxis(y, 0, 1).reshape(bsz, seq, D_MODEL)
    y = y + d_skip.astype(f32) * h.astype(f32)
    z = jax.nn.gelu(y).astype(h.dtype)
    a, b = jnp.split(z @ w_glu, 2, axis=-1)
    return a * jax.nn.sigmoid(b)


def gla_mixer(h, w_in, w_a2, b_a, norm_g, w_o):
    bsz, seq, _ = h.shape
    nc = seq // CHUNK
    f32 = jnp.float32
    proj = h @ w_in
    q, k, v, g, a_lo = jnp.split(proj, [GLA_DK, 2 * GLA_DK, 2 * GLA_DK + GLA_DV, 2 * GLA_DK + 2 * GLA_DV], axis=-1)
    log_a = jax.nn.log_sigmoid((a_lo @ w_a2 + b_a).astype(f32)) / GLA_TEMP
    shp_k = (bsz, nc, CHUNK, GLA_HEADS, GLA_DKH)
    q = q.astype(f32).reshape(shp_k) * GLA_DKH ** -0.5
    k = k.astype(f32).reshape(shp_k)
    v = v.astype(f32).reshape(bsz, nc, CHUNK, GLA_HEADS, GLA_DVH)
    cum = jnp.cumsum(log_a.reshape(shp_k), axis=2)
    tot = cum[:, :, -1]
    k_dec = k * jnp.exp(tot[:, :, None] - cum)
    upd = jnp.einsum('bnchk,bnchv->bnhkv', k_dec, v)
    decay = jnp.exp(tot)

    def chunk_step(state, xs):
        dec_c, upd_c = xs
        state = dec_c[..., None] * state + upd_c
        return state, state

    init = jnp.zeros((bsz, GLA_HEADS, GLA_DKH, GLA_DVH), f32)
    _, states = lax.scan(chunk_step, init, (jnp.moveaxis(decay, 1, 0), jnp.moveaxis(upd, 1, 0)))
    states = jnp.moveaxis(states, 0, 1)
    o = jnp.einsum('bnchk,bnhkv->bnchv', q, states).reshape(bsz, seq, GLA_HEADS, GLA_DVH)
    o = o * lax.rsqrt(jnp.mean(o * o, axis=-1, keepdims=True) + NORM_EPS) * norm_g.astype(f32).reshape(GLA_HEADS, GLA_DVH)
    o = o.reshape(bsz, seq, GLA_DV) * jax.nn.silu(g.astype(f32))
    return o.astype(h.dtype) @ w_o


def partial_rope(x, positions):
    f32 = jnp.float32
    half = ROT_DIMS // 2
    inv_freq = ROPE_THETA ** (-jnp.arange(half, dtype=f32) / half)
    ang = positions.astype(f32)[..., None] * inv_freq
    cos = jnp.cos(ang)[:, :, None]
    sin = jnp.sin(ang)[:, :, None]
    xf = x.astype(f32)
    x1 = xf[..., :half]
    x2 = xf[..., half:ROT_DIMS]
    return jnp.concatenate([x1 * cos - x2 * sin, x2 * cos + x1 * sin, xf[..., ROT_DIMS:]], axis=-1)


def diff_attn_mixer(h, positions, w_qkv, lam_q1, lam_k1, lam_q2, lam_k2, subln_g, w_o, lambda_init):
    bsz, seq, _ = h.shape
    f32 = jnp.float32
    q, k, v = jnp.split(h @ w_qkv, 3, axis=-1)
    q = partial_rope(q.reshape(bsz, seq, 2 * DIFF_HEADS, DIFF_HD), positions) * DIFF_HD ** -0.5
    k = partial_rope(k.reshape(bsz, seq, 2 * DIFF_HEADS, DIFF_HD), positions)
    v = v.astype(f32).reshape(bsz, seq, DIFF_HEADS, 2 * DIFF_HD)
    lam = (jnp.exp(jnp.sum(lam_q1.astype(f32) * lam_k1.astype(f32)))
           - jnp.exp(jnp.sum(lam_q2.astype(f32) * lam_k2.astype(f32))) + lambda_init)
    chunk_id = positions // CHUNK
    nqb = seq // Q_BLOCK
    q_blk = jnp.moveaxis(q.reshape(bsz, nqb, Q_BLOCK, 2 * DIFF_HEADS, DIFF_HD), 1, 0)
    cid_blk = jnp.moveaxis(chunk_id.reshape(bsz, nqb, Q_BLOCK), 1, 0)

    def attend_block(args):
        qb, qcid = args
        s = jnp.einsum('bqhd,bkhd->bhqk', qb, k)
        mask = chunk_id[:, None, None, :] <= qcid[:, None, :, None]
        p = jax.nn.softmax(jnp.where(mask, s, NEG_INF), axis=-1)
        p = p.reshape(bsz, DIFF_HEADS, 2, Q_BLOCK, seq)
        a = p[:, :, 0] - lam * p[:, :, 1]
        return jnp.einsum('bhqk,bkhd->bqhd', a, v)

    o = lax.map(attend_block, (q_blk, cid_blk))
    o = jnp.moveaxis(o, 0, 1).reshape(bsz, seq, DIFF_HEADS, 2 * DIFF_HD)
    o = o * lax.rsqrt(jnp.mean(o * o, axis=-1, keepdims=True) + 1e-5) * subln_g.astype(f32)
    o = o * (1.0 - lambda_init)
    return o.reshape(bsz, seq, D_MODEL).astype(h.dtype) @ w_o


def setup_inputs(seed: int = 0) -> dict:
    key = jax.random.key(seed)
    ks = iter(jax.random.split(key, 32))
    nrm = lambda shape, scale: scale * jax.random.normal(next(ks), shape, jnp.float32)
    x = jax.random.normal(next(ks), (BATCH, SEQ, D_MODEL), jnp.float32)
    positions = jnp.broadcast_to(jnp.arange(SEQ, dtype=jnp.int32), (BATCH, SEQ))
    norm_mix = 1.0 + nrm((DEPTH, D_MODEL), 0.02)
    norm_ffn = 1.0 + nrm((DEPTH, D_MODEL), 0.02)
    norm_final = 1.0 + nrm((D_MODEL,), 0.02)
    s5_lam_re = -0.5 * jnp.exp(nrm((N_S5_LAYERS, S5_GROUPS, S5_STATE), 0.05))
    s5_lam_im = jnp.pi * jnp.arange(S5_STATE, dtype=jnp.float32) + nrm((N_S5_LAYERS, S5_GROUPS, S5_STATE), 0.01)
    s5_log_dt = jax.random.uniform(next(ks), (N_S5_LAYERS, S5_GROUPS), jnp.float32, math.log(DT_MIN), math.log(DT_MAX))
    s5_b_re = nrm((N_S5_LAYERS, S5_GROUPS, S5_STATE, S5_GROUP), (2 * S5_GROUP) ** -0.5)
    s5_b_im = nrm((N_S5_LAYERS, S5_GROUPS, S5_STATE, S5_GROUP), (2 * S5_GROUP) ** -0.5)
    s5_c_re = nrm((N_S5_LAYERS, S5_GROUPS, S5_GROUP, S5_STATE), S5_STATE ** -0.5)
    s5_c_im = nrm((N_S5_LAYERS, S5_GROUPS, S5_GROUP, S5_STATE), S5_STATE ** -0.5)
    s5_d = nrm((N_S5_LAYERS, D_MODEL), 1.0)
    s5_w_glu = nrm((N_S5_LAYERS, D_MODEL, 2 * D_MODEL), D_MODEL ** -0.5)
    gla_w_in = nrm((N_GLA_LAYERS, D_MODEL, GLA_IN), D_MODEL ** -0.5)
    gla_w_a2 = nrm((N_GLA_LAYERS, GLA_GATE_RANK, GLA_DK), GLA_GATE_RANK ** -0.5)
    gla_b_a = nrm((N_GLA_LAYERS, GLA_DK), 0.1)
    gla_norm = 1.0 + nrm((N_GLA_LAYERS, GLA_DV), 0.02)
    gla_w_o = nrm((N_GLA_LAYERS, GLA_DV, D_MODEL), GLA_DV ** -0.5)
    diff_w_qkv = nrm((N_DIFF_LAYERS, D_MODEL, 3 * D_MODEL), D_MODEL ** -0.5)
    diff_lam_q1 = nrm((N_DIFF_LAYERS, DIFF_HD), 0.1)
    diff_lam_k1 = nrm((N_DIFF_LAYERS, DIFF_HD), 0.1)
    diff_lam_q2 = nrm((N_DIFF_LAYERS, DIFF_HD), 0.1)
    diff_lam_k2 = nrm((N_DIFF_LAYERS, DIFF_HD), 0.1)
    diff_subln = 1.0 + nrm((N_DIFF_LAYERS, 2 * DIFF_HD), 0.02)
    diff_w_o = nrm((N_DIFF_LAYERS, D_MODEL, D_MODEL), D_MODEL ** -0.5)
    ffn_w_gate_up = nrm((DEPTH, D_MODEL, 2 * FFN_HIDDEN), D_MODEL ** -0.5)
    ffn_w_down = nrm((DEPTH, FFN_HIDDEN, D_MODEL), FFN_HIDDEN ** -0.5)
    return {'x': x, 'positions': positions, 'norm_mix': norm_mix, 'norm_ffn': norm_ffn, 'norm_final': norm_final,
            's5_lam_re': s5_lam_re, 's5_lam_im': s5_lam_im, 's5_log_dt': s5_log_dt, 's5_b_re': s5_b_re, 's5_b_im': s5_b_im,
            's5_c_re': s5_c_re, 's5_c_im': s5_c_im, 's5_d': s5_d, 's5_w_glu': s5_w_glu,
            'gla_w_in': gla_w_in, 'gla_w_a2': gla_w_a2, 'gla_b_a': gla_b_a, 'gla_norm': gla_norm, 'gla_w_o': gla_w_o,
            'diff_w_qkv': diff_w_qkv, 'diff_lam_q1': diff_lam_q1, 'diff_lam_k1': diff_lam_k1, 'diff_lam_q2': diff_lam_q2,
            'diff_lam_k2': diff_lam_k2, 'diff_subln': diff_subln, 'diff_w_o': diff_w_o,
            'ffn_w_gate_up': ffn_w_gate_up, 'ffn_w_down': ffn_w_down}


def reference(x, positions, norm_mix, norm_ffn, norm_final,
              s5_lam_re, s5_lam_im, s5_log_dt, s5_b_re, s5_b_im, s5_c_re, s5_c_im, s5_d, s5_w_glu,
              gla_w_in, gla_w_a2, gla_b_a, gla_norm, gla_w_o,
              diff_w_qkv, diff_lam_q1, diff_lam_k1, diff_lam_q2, diff_lam_k2, diff_subln, diff_w_o,
              ffn_w_gate_up, ffn_w_down):
    for layer in range(DEPTH):
        kind = layer % N_MIXERS
        idx = layer // N_MIXERS
        h = rmsnorm(x, norm_mix[layer])
        if kind == 0:
            mix = s5_mixer(h, s5_lam_re[idx], s5_lam_im[idx], s5_log_dt[idx], s5_b_re[idx], s5_b_im[idx],
                           s5_c_re[idx], s5_c_im[idx], s5_d[idx], s5_w_glu[idx])
        elif kind == 1:
            mix = gla_mixer(h, gla_w_in[idx], gla_w_a2[idx], gla_b_a[idx], gla_norm[idx], gla_w_o[idx])
        else:
            lambda_init = 0.8 - 0.6 * math.exp(-0.3 * layer)
            mix = diff_attn_mixer(h, positions, diff_w_qkv[idx], diff_lam_q1[idx], diff_lam_k1[idx],
                                  diff_lam_q2[idx], diff_lam_k2[idx], diff_subln[idx], diff_w_o[idx], lambda_init)
        x = x + mix.astype(x.dtype)
        h = rmsnorm(x, norm_ffn[layer])
        x = x + swiglu_ffn(h, ffn_w_gate_up[layer], ffn_w_down[layer]).astype(x.dtype)
    return rmsnorm(x, norm_final)
```

```python
import functools
import math

import jax
import jax.numpy as jnp
from jax import lax
from jax.experimental import pallas as pl
from jax.experimental.pallas import tpu as pltpu

F32 = jnp.float32
BF16 = jnp.bfloat16

D_MODEL = 1024
DEPTH = 4
CHUNK = 64
N_MIXERS = 3
NORM_EPS = 1e-6
S5_GROUP = 16
S5_GROUPS = D_MODEL // S5_GROUP
S5_STATE = 64
S5_SUB = 8
S5_GB = 8
GLA_HEADS = 4
GLA_DK = D_MODEL // 2
GLA_DV = D_MODEL
GLA_DKH = GLA_DK // GLA_HEADS
GLA_DVH = GLA_DV // GLA_HEADS
GLA_GATE_RANK = 16
GLA_TEMP = 16.0
DIFF_HD = 64
DIFF_HEADS = D_MODEL // (2 * DIFF_HD)
ROT_DIMS = DIFF_HD // 4
ROPE_THETA = 500000.0
NEG_INF = -1e30
SUBLN_EPS = 1e-5
FFN_HIDDEN = -(-8 * D_MODEL // (3 * 256)) * 256

LANES = 128
VMEM_LIMIT_BYTES = 56 * 1024 * 1024


def _params(*sem):
    return pltpu.CompilerParams(dimension_semantics=sem, vmem_limit_bytes=VMEM_LIMIT_BYTES)


def _rms(x, g, eps=NORM_EPS):
    return x * lax.rsqrt(jnp.mean(x * x, axis=-1, keepdims=True) + eps) * g


def _dot(a, b):
    return jnp.dot(a, b, preferred_element_type=F32)


def _dot_nt(a, b):
    return lax.dot_general(a, b, (((1,), (1,)), ((), ())), preferred_element_type=F32)


def _dot_tn(a, b):
    return lax.dot_general(a, b, (((0,), (0,)), ((), ())), preferred_element_type=F32)


def _ffn_kernel(x_ref, g_ref, wg_ref, wu_ref, wd_ref, gf_ref, o_ref, h_scr, acc_scr, *, nk, final):
    k = pl.program_id(1)

    @pl.when(k == 0)
    def _():
        h_scr[...] = _rms(x_ref[...], g_ref[...]).astype(BF16)
        acc_scr[...] = jnp.zeros_like(acc_scr)

    h = h_scr[...]
    gate = _dot(h, wg_ref[...])
    up = _dot(h, wu_ref[...])
    act = (gate * jax.nn.sigmoid(gate) * up).astype(BF16)
    acc_scr[...] += _dot(act, wd_ref[...])

    @pl.when(k == nk - 1)
    def _():
        y = x_ref[...] + acc_scr[...]
        if final:
            y = _rms(y, gf_ref[...])
        o_ref[...] = y


def _ffn_layer(x, g, wgu, wd, gf, *, final, tm, th):
    L = x.shape[0]
    nk = FFN_HIDDEN // th
    return pl.pallas_call(
        functools.partial(_ffn_kernel, nk=nk, final=final),
        grid=(L // tm, nk),
        in_specs=[
            pl.BlockSpec((tm, D_MODEL), lambda i, k: (i, 0)),
            pl.BlockSpec((1, D_MODEL), lambda i, k: (0, 0)),
            pl.BlockSpec((D_MODEL, th), lambda i, k: (0, k)),
            pl.BlockSpec((D_MODEL, th), lambda i, k: (0, k + nk)),
            pl.BlockSpec((th, D_MODEL), lambda i, k: (k, 0)),
            pl.BlockSpec((1, D_MODEL), lambda i, k: (0, 0)),
        ],
        out_specs=pl.BlockSpec((tm, D_MODEL), lambda i, k: (i, 0)),
        out_shape=jax.ShapeDtypeStruct((L, D_MODEL), F32),
        scratch_shapes=[pltpu.VMEM((tm, D_MODEL), BF16), pltpu.VMEM((tm, D_MODEL), F32)],
        compiler_params=_params("parallel", "arbitrary"),
        name="ffn",
    )(x, g, wgu, wgu, wd, gf)


def _s5_tables(lam_re, lam_im, log_dt, b_re, b_im, c_re, c_im):
    G, P, H, T, NB = S5_GROUPS, S5_STATE, S5_GROUP, S5_SUB, S5_GB
    f32 = F32
    lr = lam_re.astype(f32)
    li = lam_im.astype(f32)
    dt = jnp.exp(log_dt.astype(f32))[:, None]
    ab_mag = jnp.exp(lr * dt)
    ab_ang = li * dt
    ab_re = ab_mag * jnp.cos(ab_ang)
    ab_im = ab_mag * jnp.sin(ab_ang)
    den = lr * lr + li * li
    f_re = ((ab_re - 1.0) * lr + ab_im * li) / den
    f_im = (ab_im * lr - (ab_re - 1.0) * li) / den
    br = b_re.astype(f32)
    bi = b_im.astype(f32)
    bb_re = f_re[..., None] * br - f_im[..., None] * bi
    bb_im = f_re[..., None] * bi + f_im[..., None] * br
    cr = c_re.astype(f32)
    ci = c_im.astype(f32)

    def powers(n):
        n = n.astype(f32)[:, None, None]
        mag = jnp.exp(lr * dt * n)
        ang = li * dt * n
        return mag * jnp.cos(ang), mag * jnp.sin(ang)

    pr, pi = powers(jnp.arange(T + 1))
    cp_re = cr[None] * pr[:, :, None, :] - ci[None] * pi[:, :, None, :]
    cp_im = cr[None] * pi[:, :, None, :] + ci[None] * pr[:, :, None, :]
    hp = lax.Precision.HIGHEST
    kern = (jnp.einsum("ngkp,gph->ngkh", cp_re[:T], bb_re, precision=hp)
            - jnp.einsum("ngkp,gph->ngkh", cp_im[:T], bb_im, precision=hp))
    eye = jnp.eye(NB, dtype=f32)

    t_in = jnp.arange(T)[:, None]
    t_out = jnp.arange(T)[None, :]
    lag = jnp.clip(t_out - t_in, 0, T - 1)
    causal = (t_out >= t_in).astype(f32)
    kt = kern[lag] * causal[:, :, None, None, None]
    kt = kt.transpose(2, 0, 4, 1, 3).reshape(NB, NB, T, H, T, H)
    w = kt[:, :, :, :, :, None, :] * eye[None, :, None, None, None, :, None]
    w = w.transpose(0, 2, 1, 3, 4, 5, 6).reshape(NB, T * NB * H, T * NB * H)

    prq = pr[T - 1 - jnp.arange(T)]
    piq = pi[T - 1 - jnp.arange(T)]
    qv_re = prq[..., None] * bb_re[None] - piq[..., None] * bb_im[None]
    qv_im = prq[..., None] * bb_im[None] + piq[..., None] * bb_re[None]
    qv = jnp.stack([qv_re, qv_im]).reshape(2, T, NB, NB, P, H)
    qv = qv.transpose(2, 1, 3, 5, 0, 4)
    q = qv[:, :, :, :, :, None, :] * eye[None, None, :, None, None, :, None]
    q = q.reshape(NB, T * NB * H, 2 * NB * P)

    pm = jnp.stack([cp_re[1:], -cp_im[1:]]).reshape(2, T, NB, NB, H, P)
    pm = pm.transpose(2, 0, 3, 5, 1, 4)
    p = pm[:, :, :, :, :, None, :] * eye[None, None, :, None, None, :, None]
    p = p.reshape(NB, 2 * NB * P, T * NB * H)

    sr, si = powers(T * jnp.arange(1, 9))
    rows = jnp.arange(8)
    kinds = []
    for sh in (1, 2, 4):
        m = (rows >= sh).astype(f32)[:, None, None]
        kinds += [m * sr[sh - 1][None], m * si[sh - 1][None]]
    kinds += [sr, si]
    mult = jnp.stack(kinds)
    mult = mult.reshape(8, 8, NB, NB * P).transpose(2, 0, 1, 3).reshape(NB, 64, NB * P)
    return w.astype(BF16), q.astype(BF16), p.astype(BF16), mult


def _rms_kernel(x_ref, g_ref, o_ref):
    o_ref[...] = _rms(x_ref[...], g_ref[...])


def _rms_layer(x, g, *, tm):
    L = x.shape[0]
    return pl.pallas_call(
        _rms_kernel,
        grid=(L // tm,),
        in_specs=[pl.BlockSpec((tm, D_MODEL), lambda i: (i, 0)),
                  pl.BlockSpec((1, D_MODEL), lambda i: (0, 0))],
        out_specs=pl.BlockSpec((tm, D_MODEL), lambda i: (i, 0)),
        out_shape=jax.ShapeDtypeStruct((L, D_MODEL), F32),
        compiler_params=_params("parallel"),
        name="s5_norm",
    )(x, g)


def _s5_core_kernel(h_ref, w_ref, q_ref, p_ref, m_ref, y_ref, carry_scr, b_scr, s_scr, *, rows):
    half = S5_GB * S5_STATE

    @pl.when(pl.program_id(1) == 0)
    def _():
        carry_scr[...] = jnp.zeros_like(carry_scr)

    x = jnp.concatenate(
        [h_ref[pl.ds(t, rows, stride=S5_SUB), :] for t in range(S5_SUB)], axis=1).astype(BF16)
    b_scr[...] = _dot(x, q_ref[0])

    first_row = lax.broadcasted_iota(jnp.int32, (8, half), 0) == 0
    c_re = carry_scr[:, :half]
    c_im = carry_scr[:, half:]
    for tile in range(rows // 8):
        r0 = tile * 8
        x_re = b_scr[r0:r0 + 8, :half]
        x_im = b_scr[r0:r0 + 8, half:]
        for lvl, sh in enumerate((1, 2, 4)):
            a_re = m_ref[0, 16 * lvl:16 * lvl + 8, :]
            a_im = m_ref[0, 16 * lvl + 8:16 * lvl + 16, :]
            r_re = pltpu.roll(x_re, sh, 0)
            r_im = pltpu.roll(x_im, sh, 0)
            x_re, x_im = (x_re + a_re * r_re - a_im * r_im,
                          x_im + a_re * r_im + a_im * r_re)
        p_re = m_ref[0, 48:56, :]
        p_im = m_ref[0, 56:64, :]
        s_re = x_re + p_re * c_re - p_im * c_im
        s_im = x_im + p_re * c_im + p_im * c_re
        s_scr[r0:r0 + 8, :half] = jnp.where(first_row, c_re, pltpu.roll(s_re, 1, 0))
        s_scr[r0:r0 + 8, half:] = jnp.where(first_row, c_im, pltpu.roll(s_im, 1, 0))
        c_re = jnp.broadcast_to(s_re[7:8, :], (8, half))
        c_im = jnp.broadcast_to(s_im[7:8, :], (8, half))
    carry_scr[:, :half] = c_re
    carry_scr[:, half:] = c_im

    y = _dot(x, w_ref[0]) + _dot(s_scr[...].astype(BF16), p_ref[0])
    for t in range(S5_SUB):
        y_ref[pl.ds(t, rows, stride=S5_SUB), :] = y[:, t * LANES:(t + 1) * LANES]


def _s5_core(h, w, q, p, mult, *, tm):
    L = h.shape[0]
    rows = tm // S5_SUB
    wide = S5_SUB * LANES
    return pl.pallas_call(
        functools.partial(_s5_core_kernel, rows=rows),
        grid=(S5_GB, L // tm),
        in_specs=[
            pl.BlockSpec((tm, LANES), lambda g, i: (i, g)),
            pl.BlockSpec((1, wide, wide), lambda g, i: (g, 0, 0)),
            pl.BlockSpec((1, wide, wide), lambda g, i: (g, 0, 0)),
            pl.BlockSpec((1, wide, wide), lambda g, i: (g, 0, 0)),
            pl.BlockSpec((1, 64, S5_GB * S5_STATE), lambda g, i: (g, 0, 0)),
        ],
        out_specs=pl.BlockSpec((tm, LANES), lambda g, i: (i, g)),
        out_shape=jax.ShapeDtypeStruct((L, D_MODEL), F32),
        scratch_shapes=[pltpu.VMEM((8, wide), F32), pltpu.VMEM((rows, wide), F32),
                        pltpu.VMEM((rows, wide), F32)],
        compiler_params=_params("parallel", "arbitrary"),
        name="s5_core",
    )(h, w, q, p, mult)


def _gelu_tanh(x):
    return x * (0.5 * (1.0 + jnp.tanh(math.sqrt(2.0 / math.pi) * (x + 0.044715 * (x * x * x)))))


def _s5_post_kernel(x_ref, h_ref, y_ref, d_ref, w_ref, o_ref):
    y = y_ref[...] + d_ref[...] * h_ref[...]
    z = _gelu_tanh(y).astype(BF16)
    ab = _dot(z, w_ref[...])
    o_ref[...] = x_ref[...] + ab[:, :D_MODEL] * jax.nn.sigmoid(ab[:, D_MODEL:])


def _s5_post(x, h, y, d, w_glu, *, tm):
    L = x.shape[0]
    row = pl.BlockSpec((tm, D_MODEL), lambda i: (i, 0))
    return pl.pallas_call(
        _s5_post_kernel,
        grid=(L // tm,),
        in_specs=[row, row, row,
                  pl.BlockSpec((1, D_MODEL), lambda i: (0, 0)),
                  pl.BlockSpec((D_MODEL, 2 * D_MODEL), lambda i: (0, 0))],
        out_specs=row,
        out_shape=jax.ShapeDtypeStruct((L, D_MODEL), F32),
        compiler_params=_params("parallel"),
        name="s5_post",
    )(x, h, y, d, w_glu)


def _s5_mixer_layer(x, g, lam_re, lam_im, log_dt, b_re, b_im, c_re, c_im, d_skip, w_glu, *, tm, tm_core):
    w, q, p, mult = _s5_tables(lam_re, lam_im, log_dt, b_re, b_im, c_re, c_im)
    h = _rms_layer(x, g, tm=tm)
    y = _s5_core(h, w, q, p, mult, tm=tm_core)
    return _s5_post(x, h, y, d_skip.reshape(1, D_MODEL), w_glu.astype(BF16), tm=tm)


def _gla_kernel(x_ref, g_ref, wq_ref, wk_ref, wv_ref, wg_ref, wa1_ref, wa2_ref, ba_ref, ng_ref,
                wo_ref, tri_ref, same_ref, o_ref, st_scr, o_scr, *, tm):
    @pl.when(pl.program_id(0) == 0)
    def _():
        st_scr[...] = jnp.zeros_like(st_scr)

    x = x_ref[...]
    h = _rms(x, g_ref[...]).astype(BF16)
    q = (_dot(h, wq_ref[...]) * (GLA_DKH ** -0.5)).astype(BF16)
    k = _dot(h, wk_ref[...])
    v = _dot(h, wv_ref[...]).astype(BF16)
    gate = _dot(h, wg_ref[...])
    a_lo = _dot(h, wa1_ref[...]).astype(BF16)
    log_a = jax.nn.log_sigmoid(_dot(a_lo, wa2_ref[...]) + ba_ref[...]) / GLA_TEMP

    la_hi = log_a.astype(BF16)
    la_lo = (log_a - la_hi.astype(F32)).astype(BF16)
    cum = _dot(tri_ref[...], la_hi) + _dot(tri_ref[...], la_lo)
    tot = _dot(same_ref[...], la_hi) + _dot(same_ref[...], la_lo)
    k_dec = (k * jnp.exp(tot - cum)).astype(BF16)

    for c in range(tm // CHUNK):
        r0 = c * CHUNK
        dec_row = jnp.exp(tot[r0:r0 + 1, :])
        for hd in range(GLA_HEADS):
            ks = slice(hd * GLA_DKH, (hd + 1) * GLA_DKH)
            vs = slice(hd * GLA_DVH, (hd + 1) * GLA_DVH)
            upd_t = _dot_tn(v[r0:r0 + CHUNK, vs], k_dec[r0:r0 + CHUNK, ks])
            st = st_scr[hd] * dec_row[:, ks] + upd_t
            st_scr[hd] = st
            o_scr[r0:r0 + CHUNK, vs] = _dot_nt(q[r0:r0 + CHUNK, ks], st.astype(BF16))

    outs = []
    for hd in range(GLA_HEADS):
        vs = slice(hd * GLA_DVH, (hd + 1) * GLA_DVH)
        o = o_scr[:, vs]
        outs.append(o * lax.rsqrt(jnp.mean(o * o, axis=-1, keepdims=True) + NORM_EPS) * ng_ref[:, vs])
    o = jnp.concatenate(outs, axis=1) * (gate * jax.nn.sigmoid(gate))
    o_ref[...] = x + _dot(o.astype(BF16), wo_ref[...])


def _gla_mixer_layer(x, g, w_in, w_a2, b_a, norm_g, w_o, *, tm):
    L = x.shape[0]
    w_in = w_in.astype(BF16)
    wq = w_in[:, :GLA_DK]
    wk = w_in[:, GLA_DK:2 * GLA_DK]
    wv = w_in[:, 2 * GLA_DK:2 * GLA_DK + GLA_DV]
    wg = w_in[:, 2 * GLA_DK + GLA_DV:2 * GLA_DK + 2 * GLA_DV]
    wa1 = jnp.pad(w_in[:, 2 * GLA_DK + 2 * GLA_DV:], ((0, 0), (0, LANES - GLA_GATE_RANK)))
    wa2 = jnp.pad(w_a2.astype(BF16), ((0, LANES - GLA_GATE_RANK), (0, 0)))
    r = jnp.arange(tm)
    same = (r[:, None] // CHUNK) == (r[None, :] // CHUNK)
    tri = (same & (r[None, :] <= r[:, None])).astype(BF16)
    same = same.astype(BF16)

    def full(shape):
        return pl.BlockSpec(shape, lambda i: (0,) * len(shape))

    row = pl.BlockSpec((tm, D_MODEL), lambda i: (i, 0))
    return pl.pallas_call(
        functools.partial(_gla_kernel, tm=tm),
        grid=(L // tm,),
        in_specs=[row, full((1, D_MODEL)), full((D_MODEL, GLA_DK)), full((D_MODEL, GLA_DK)),
                  full((D_MODEL, GLA_DV)), full((D_MODEL, GLA_DV)), full((D_MODEL, LANES)),
                  full((LANES, GLA_DK)), full((1, GLA_DK)), full((1, GLA_DV)),
                  full((GLA_DV, D_MODEL)), full((tm, tm)), full((tm, tm))],
        out_specs=row,
        out_shape=jax.ShapeDtypeStruct((L, D_MODEL), F32),
        scratch_shapes=[pltpu.VMEM((GLA_HEADS, GLA_DVH, GLA_DKH), F32),
                        pltpu.VMEM((tm, GLA_DV), F32)],
        compiler_params=_params("arbitrary"),
        name="gla",
    )(x, g, wq, wk, wv, wg, wa1, wa2, b_a.reshape(1, GLA_DK), norm_g.reshape(1, GLA_DV),
      w_o.astype(BF16), tri, same)


def _rope_tables(positions):
    half = ROT_DIMS // 2
    inv_freq = ROPE_THETA ** (-jnp.arange(half, dtype=F32) / half)
    ang = positions.astype(F32)[:, None] * inv_freq
    cos = jnp.cos(ang)
    sin = jnp.sin(ang)
    m = jnp.arange(LANES) % DIFF_HD
    cos_l = cos[:, m % half]
    sin_l = sin[:, m % half]
    c = jnp.where(m < ROT_DIMS, cos_l, 1.0)
    s_lo = jnp.where(m < half, -sin_l, 0.0)
    s_hi = jnp.where((m >= half) & (m < ROT_DIMS), sin_l, 0.0)
    return c, s_lo, s_hi


def _qkv_kernel(x_ref, g_ref, w_ref, c_ref, slo_ref, shi_ref, q_ref, k_ref, v_ref):
    h = _rms(x_ref[...], g_ref[...]).astype(BF16)
    qkv = _dot(h, w_ref[...])
    c = c_ref[...]
    s_lo = slo_ref[...]
    s_hi = shi_ref[...]
    half = ROT_DIMS // 2

    def rope(t):
        return t * c + pltpu.roll(t, LANES - half, 1) * s_lo + pltpu.roll(t, half, 1) * s_hi

    for j in range(D_MODEL // LANES):
        ls = slice(j * LANES, (j + 1) * LANES)
        q_ref[:, ls] = (rope(qkv[:, ls]) * (DIFF_HD ** -0.5)).astype(BF16)
        k_ref[:, ls] = rope(qkv[:, D_MODEL + j * LANES:D_MODEL + (j + 1) * LANES]).astype(BF16)
    v_ref[...] = qkv[:, 2 * D_MODEL:].astype(BF16)


def _qkv_layer(x, g, w_qkv, c, s_lo, s_hi, *, tm):
    L = x.shape[0]
    row = pl.BlockSpec((tm, D_MODEL), lambda i: (i, 0))
    tab = pl.BlockSpec((tm, LANES), lambda i: (i, 0))
    out = jax.ShapeDtypeStruct((L, D_MODEL), BF16)
    return pl.pallas_call(
        _qkv_kernel,
        grid=(L // tm,),
        in_specs=[row, pl.BlockSpec((1, D_MODEL), lambda i: (0, 0)),
                  pl.BlockSpec((D_MODEL, 3 * D_MODEL), lambda i: (0, 0)), tab, tab, tab],
        out_specs=[row, row, row],
        out_shape=[out, out, out],
        compiler_params=_params("parallel"),
        name="diff_qkv",
    )(x, g, w_qkv, c, s_lo, s_hi)


def _attn_kernel(q_ref, k_ref, v_ref, lamv_ref, sg_ref, o_ref, m_scr, l_scr, acc_scr, *,
                 tq, lambda_init):
    i = pl.program_id(1)
    j = pl.program_id(2)

    @pl.when(j == 0)
    def _():
        m_scr[...] = jnp.full_like(m_scr, NEG_INF)
        l_scr[...] = jnp.zeros_like(l_scr)
        acc_scr[...] = jnp.zeros_like(acc_scr)

    def step(masked):
        q = q_ref[...]
        k = k_ref[...]
        v = v_ref[...]
        lane = lax.broadcasted_iota(jnp.int32, q.shape, 1)
        if masked:
            r = lax.broadcasted_iota(jnp.int32, (tq, tq), 0) // CHUNK
            c = lax.broadcasted_iota(jnp.int32, (tq, tq), 1) // CHUNK
            visible = c <= r
        for hf in range(2):
            in_half = (lane >= DIFF_HD) if hf else (lane < DIFF_HD)
            qh = jnp.where(in_half, q, jnp.zeros_like(q))
            s = _dot_nt(qh, k)
            if masked:
                s = jnp.where(visible, s, NEG_INF)
            m_prev = m_scr[hf]
            m_new = jnp.maximum(m_prev, jnp.max(s, axis=-1, keepdims=True))
            alpha = jnp.exp(m_prev - m_new)
            p = jnp.exp(s - m_new)
            l_scr[hf] = alpha * l_scr[hf] + jnp.sum(p, axis=-1, keepdims=True)
            acc_scr[hf] = alpha * acc_scr[hf] + _dot(p.astype(BF16), v)
            m_scr[hf] = m_new

    @pl.when(j < i)
    def _():
        step(False)

    @pl.when(j == i)
    def _():
        step(True)
        lv = lamv_ref[...]
        lam = (jnp.exp(jnp.sum(lv[0:1] * lv[1:2], axis=-1, keepdims=True))
               - jnp.exp(jnp.sum(lv[2:3] * lv[3:4], axis=-1, keepdims=True)) + lambda_init)
        o = acc_scr[0] / l_scr[0] - lam * (acc_scr[1] / l_scr[1])
        o = o * lax.rsqrt(jnp.mean(o * o, axis=-1, keepdims=True) + SUBLN_EPS) * sg_ref[...]
        o_ref[...] = (o * (1.0 - lambda_init)).astype(BF16)


def _attn_layer(q, k, v, lamv, subln_g, *, tq, lambda_init):
    L = q.shape[0]
    n = L // tq
    return pl.pallas_call(
        functools.partial(_attn_kernel, tq=tq, lambda_init=lambda_init),
        grid=(DIFF_HEADS, n, n),
        in_specs=[
            pl.BlockSpec((tq, LANES), lambda p, i, j: (i, p)),
            pl.BlockSpec((tq, LANES), lambda p, i, j: (jnp.minimum(j, i), p)),
            pl.BlockSpec((tq, LANES), lambda p, i, j: (jnp.minimum(j, i), p)),
            pl.BlockSpec((4, DIFF_HD), lambda p, i, j: (0, 0)),
            pl.BlockSpec((1, LANES), lambda p, i, j: (0, 0)),
        ],
        out_specs=pl.BlockSpec((tq, LANES), lambda p, i, j: (i, p)),
        out_shape=jax.ShapeDtypeStruct((L, D_MODEL), BF16),
        scratch_shapes=[pltpu.VMEM((2, tq, 1), F32), pltpu.VMEM((2, tq, 1), F32),
                        pltpu.VMEM((2, tq, LANES), F32)],
        compiler_params=_params("parallel", "parallel", "arbitrary"),
        name="diff_attn",
    )(q, k, v, lamv, subln_g)


def _proj_kernel(x_ref, a_ref, w_ref, o_ref):
    o_ref[...] = x_ref[...] + _dot(a_ref[...], w_ref[...])


def _proj_layer(x, a, w, *, tm):
    L = x.shape[0]
    row = pl.BlockSpec((tm, D_MODEL), lambda i: (i, 0))
    return pl.pallas_call(
        _proj_kernel,
        grid=(L // tm,),
        in_specs=[row, row, pl.BlockSpec((D_MODEL, D_MODEL), lambda i: (0, 0))],
        out_specs=row,
        out_shape=jax.ShapeDtypeStruct((L, D_MODEL), F32),
        compiler_params=_params("parallel"),
        name="diff_out",
    )(x, a, w)


def _diff_mixer_layer(x, g, positions, w_qkv, lam_q1, lam_k1, lam_q2, lam_k2, subln_g, w_o,
                      lambda_init, *, tm, tq):
    c, s_lo, s_hi = _rope_tables(positions)
    q, k, v = _qkv_layer(x, g, w_qkv.astype(BF16), c, s_lo, s_hi, tm=tm)
    lamv = jnp.stack([lam_q1, lam_k1, lam_q2, lam_k2]).astype(F32)
    a = _attn_layer(q, k, v, lamv, subln_g.reshape(1, LANES).astype(F32), tq=tq,
                    lambda_init=lambda_init)
    return _proj_layer(x, a, w_o.astype(BF16), tm=tm)


def _trunk(x, positions, norm_mix, norm_ffn, norm_final,
           s5_lam_re, s5_lam_im, s5_log_dt, s5_b_re, s5_b_im, s5_c_re, s5_c_im, s5_d, s5_w_glu,
           gla_w_in, gla_w_a2, gla_b_a, gla_norm, gla_w_o,
           diff_w_qkv, diff_lam_q1, diff_lam_k1, diff_lam_q2, diff_lam_k2, diff_subln, diff_w_o,
           ffn_w_gate_up, ffn_w_down, *, tm, tm_core, tm_gla, tq, tm_ffn, th):
    depth = norm_mix.shape[0]
    gfinal = norm_final.reshape(1, D_MODEL)
    for layer in range(depth):
        kind = layer % N_MIXERS
        idx = layer // N_MIXERS
        g = norm_mix[layer].reshape(1, D_MODEL)
        if kind == 0:
            x = _s5_mixer_layer(x, g, s5_lam_re[idx], s5_lam_im[idx], s5_log_dt[idx], s5_b_re[idx],
                                s5_b_im[idx], s5_c_re[idx], s5_c_im[idx], s5_d[idx], s5_w_glu[idx],
                                tm=tm, tm_core=tm_core)
        elif kind == 1:
            x = _gla_mixer_layer(x, g, gla_w_in[idx], gla_w_a2[idx], gla_b_a[idx], gla_norm[idx],
                                 gla_w_o[idx], tm=tm_gla)
        else:
            lambda_init = 0.8 - 0.6 * math.exp(-0.3 * layer)
            x = _diff_mixer_layer(x, g, positions, diff_w_qkv[idx], diff_lam_q1[idx],
                                  diff_lam_k1[idx], diff_lam_q2[idx], diff_lam_k2[idx],
                                  diff_subln[idx], diff_w_o[idx], lambda_init, tm=tm, tq=tq)
        x = _ffn_layer(x, norm_ffn[layer].reshape(1, D_MODEL), ffn_w_gate_up[layer].astype(BF16),
                       ffn_w_down[layer].astype(BF16), gfinal, final=(layer == depth - 1),
                       tm=tm_ffn, th=th)
    return x


def kernel(x, positions, norm_mix, norm_ffn, norm_final, s5_lam_re, s5_lam_im, s5_log_dt, s5_b_re, s5_b_im, s5_c_re, s5_c_im, s5_d, s5_w_glu, gla_w_in, gla_w_a2, gla_b_a, gla_norm, gla_w_o, diff_w_qkv, diff_lam_q1, diff_lam_k1, diff_lam_q2, diff_lam_k2, diff_subln, diff_w_o, ffn_w_gate_up, ffn_w_down):
    bsz, seq, _ = x.shape
    outs = []
    for b in range(bsz):
        outs.append(_trunk(
            x[b], positions[b], norm_mix, norm_ffn, norm_final,
            s5_lam_re, s5_lam_im, s5_log_dt, s5_b_re, s5_b_im, s5_c_re, s5_c_im, s5_d, s5_w_glu,
            gla_w_in, gla_w_a2, gla_b_a, gla_norm, gla_w_o,
            diff_w_qkv, diff_lam_q1, diff_lam_k1, diff_lam_q2, diff_lam_k2, diff_subln, diff_w_o,
            ffn_w_gate_up, ffn_w_down,
            tm=512, tm_core=4096, tm_gla=512, tq=512, tm_ffn=1024, th=256))
    return jnp.stack(outs)
```

```python
import functools
import math

import jax
import jax.numpy as jnp
from jax import lax
from jax.experimental import pallas as pl
from jax.experimental.pallas import tpu as pltpu

F32 = jnp.float32
BF16 = jnp.bfloat16

D_MODEL = 1024
DEPTH = 4
CHUNK = 64
N_MIXERS = 3
NORM_EPS = 1e-6
S5_GROUP = 16
S5_GROUPS = D_MODEL // S5_GROUP
S5_STATE = 64
S5_SUB = 8
S5_GB = 8
GLA_HEADS = 4
GLA_DK = D_MODEL // 2
GLA_DV = D_MODEL
GLA_DKH = GLA_DK // GLA_HEADS
GLA_DVH = GLA_DV // GLA_HEADS
GLA_GATE_RANK = 16
GLA_TEMP = 16.0
DIFF_HD = 64
DIFF_HEADS = D_MODEL // (2 * DIFF_HD)
ROT_DIMS = DIFF_HD // 4
ROPE_THETA = 500000.0
NEG_INF = -1e30
SUBLN_EPS = 1e-5
LOG2E = math.log2(math.e)
FFN_HIDDEN = -(-8 * D_MODEL // (3 * 256)) * 256

LANES = 128
VMEM_LIMIT_BYTES = 56 * 1024 * 1024


def _params(*sem):
    return pltpu.CompilerParams(dimension_semantics=sem, vmem_limit_bytes=VMEM_LIMIT_BYTES)


def _rms(x, g, eps=NORM_EPS):
    return x * lax.rsqrt(jnp.mean(x * x, axis=-1, keepdims=True) + eps) * g


def _dot(a, b):
    return jnp.dot(a, b, preferred_element_type=F32)


def _dot_nt(a, b):
    return lax.dot_general(a, b, (((1,), (1,)), ((), ())), preferred_element_type=F32)


def _dot_tn(a, b):
    return lax.dot_general(a, b, (((0,), (0,)), ((), ())), preferred_element_type=F32)


def _ffn_kernel(x_ref, g_ref, wg_ref, wu_ref, wd_ref, gf_ref, o_ref, h_scr, acc_scr, *, nk, final):
    k = pl.program_id(1)

    @pl.when(k == 0)
    def _():
        h_scr[...] = _rms(x_ref[...], g_ref[...]).astype(BF16)
        acc_scr[...] = jnp.zeros_like(acc_scr)

    h = h_scr[...]
    gate = _dot(h, wg_ref[...])
    up = _dot(h, wu_ref[...])
    act = (gate * jax.nn.sigmoid(gate) * up).astype(BF16)
    acc_scr[...] += _dot(act, wd_ref[...])

    @pl.when(k == nk - 1)
    def _():
        y = x_ref[...] + acc_scr[...]
        if final:
            y = _rms(y, gf_ref[...])
        o_ref[...] = y


def _ffn_layer(x, g, wgu, wd, gf, *, final, tm, th):
    L = x.shape[0]
    nk = FFN_HIDDEN // th
    return pl.pallas_call(
        functools.partial(_ffn_kernel, nk=nk, final=final),
        grid=(L // tm, nk),
        in_specs=[
            pl.BlockSpec((tm, D_MODEL), lambda i, k: (i, 0)),
            pl.BlockSpec((1, D_MODEL), lambda i, k: (0, 0)),
            pl.BlockSpec((D_MODEL, th), lambda i, k: (0, k)),
            pl.BlockSpec((D_MODEL, th), lambda i, k: (0, k + nk)),
            pl.BlockSpec((th, D_MODEL), lambda i, k: (k, 0)),
            pl.BlockSpec((1, D_MODEL), lambda i, k: (0, 0)),
        ],
        out_specs=pl.BlockSpec((tm, D_MODEL), lambda i, k: (i, 0)),
        out_shape=jax.ShapeDtypeStruct((L, D_MODEL), F32),
        scratch_shapes=[pltpu.VMEM((tm, D_MODEL), BF16), pltpu.VMEM((tm, D_MODEL), F32)],
        compiler_params=_params("parallel", "arbitrary"),
        name="ffn",
    )(x, g, wgu, wgu, wd, gf)


def _s5_tables(lam_re, lam_im, log_dt, b_re, b_im, c_re, c_im):
    G, P, H, T, NB = S5_GROUPS, S5_STATE, S5_GROUP, S5_SUB, S5_GB
    f32 = F32
    lr = lam_re.astype(f32)
    li = lam_im.astype(f32)
    dt = jnp.exp(log_dt.astype(f32))[:, None]
    ab_mag = jnp.exp(lr * dt)
    ab_ang = li * dt
    ab_re = ab_mag * jnp.cos(ab_ang)
    ab_im = ab_mag * jnp.sin(ab_ang)
    den = lr * lr + li * li
    f_re = ((ab_re - 1.0) * lr + ab_im * li) / den
    f_im = (ab_im * lr - (ab_re - 1.0) * li) / den
    br = b_re.astype(f32)
    bi = b_im.astype(f32)
    bb_re = f_re[..., None] * br - f_im[..., None] * bi
    bb_im = f_re[..., None] * bi + f_im[..., None] * br
    cr = c_re.astype(f32)
    ci = c_im.astype(f32)

    def powers(n):
        n = n.astype(f32)[:, None, None]
        mag = jnp.exp(lr * dt * n)
        ang = li * dt * n
        return mag * jnp.cos(ang), mag * jnp.sin(ang)

    pr, pi = powers(jnp.arange(T + 1))
    cp_re = cr[None] * pr[:, :, None, :] - ci[None] * pi[:, :, None, :]
    cp_im = cr[None] * pi[:, :, None, :] + ci[None] * pr[:, :, None, :]
    hp = lax.Precision.HIGHEST
    kern = (jnp.einsum("ngkp,gph->ngkh", cp_re[:T], bb_re, precision=hp)
            - jnp.einsum("ngkp,gph->ngkh", cp_im[:T], bb_im, precision=hp))

    def blockdiag(xc, rows_per, cols_per):
        r = jnp.arange(NB * rows_per)[:, None] // rows_per
        c = jnp.arange(NB * cols_per)[None, :] // cols_per
        tiled = jnp.tile(xc, (1,) * (xc.ndim - 1) + (NB,))
        return jnp.where(r == c, tiled, 0.0).astype(BF16)

    t_in = jnp.arange(T)[:, None]
    t_out = jnp.arange(T)[None, :]
    lag = jnp.clip(t_out - t_in, 0, T - 1)
    causal = t_out >= t_in
    xk = kern.transpose(1, 0, 3, 2).reshape(NB, NB, T, H, H)
    xk = xk.transpose(0, 2, 1, 3, 4).reshape(NB, T, NB * H, H)
    bd = blockdiag(xk, H, H)
    w = jnp.where(causal[None, :, :, None, None], bd[:, lag], jnp.zeros((), BF16))
    w = w.transpose(0, 1, 3, 2, 4).reshape(NB, T * NB * H, T * NB * H)

    prq = pr[T - 1 - jnp.arange(T)]
    piq = pi[T - 1 - jnp.arange(T)]
    qv_re = prq[..., None] * bb_re[None] - piq[..., None] * bb_im[None]
    qv_im = prq[..., None] * bb_im[None] + piq[..., None] * bb_re[None]
    qv = jnp.stack([qv_re, qv_im]).reshape(2, T, NB, NB, P, H)
    qv = qv.transpose(2, 1, 0, 3, 5, 4).reshape(NB, T, 2, NB * H, P)
    q = blockdiag(qv, H, P)
    q = q.transpose(0, 1, 3, 2, 4).reshape(NB, T * NB * H, 2 * NB * P)

    pm = jnp.stack([cp_re[1:], -cp_im[1:]]).reshape(2, T, NB, NB, H, P)
    pm = pm.transpose(2, 0, 1, 3, 5, 4).reshape(NB, 2, T, NB * P, H)
    p = blockdiag(pm, P, H)
    p = p.transpose(0, 1, 3, 2, 4).reshape(NB, 2 * NB * P, T * NB * H)

    sr, si = powers(T * jnp.arange(1, 9))
    rows = jnp.arange(8)
    kinds = []
    for sh in (1, 2, 4):
        m = (rows >= sh).astype(f32)[:, None, None]
        kinds += [m * sr[sh - 1][None], m * si[sh - 1][None]]
    kinds += [sr, si]
    mult = jnp.stack(kinds)
    mult = mult.reshape(8, 8, NB, NB * P).transpose(2, 0, 1, 3).reshape(NB, 64, NB * P)
    return w, q, p, mult


def _rms_kernel(x_ref, g_ref, o_ref):
    o_ref[...] = _rms(x_ref[...], g_ref[...])


def _rms_layer(x, g, *, tm):
    L = x.shape[0]
    return pl.pallas_call(
        _rms_kernel,
        grid=(L // tm,),
        in_specs=[pl.BlockSpec((tm, D_MODEL), lambda i: (i, 0)),
                  pl.BlockSpec((1, D_MODEL), lambda i: (0, 0))],
        out_specs=pl.BlockSpec((tm, D_MODEL), lambda i: (i, 0)),
        out_shape=jax.ShapeDtypeStruct((L, D_MODEL), F32),
        compiler_params=_params("parallel"),
        name="s5_norm",
    )(x, g)


def _s5_core_kernel(h_ref, w_ref, q_ref, p_ref, m_ref, y_ref, carry_scr, b_scr, s_scr, *, rows):
    half = S5_GB * S5_STATE

    @pl.when(pl.program_id(1) == 0)
    def _():
        carry_scr[...] = jnp.zeros_like(carry_scr)

    x = jnp.concatenate(
        [h_ref[pl.ds(t, rows, stride=S5_SUB), :] for t in range(S5_SUB)], axis=1).astype(BF16)
    b_scr[...] = _dot(x, q_ref[0])

    first_row = lax.broadcasted_iota(jnp.int32, (8, half), 0) == 0
    c_re = carry_scr[:, :half]
    c_im = carry_scr[:, half:]
    for tile in range(rows // 8):
        r0 = tile * 8
        x_re = b_scr[r0:r0 + 8, :half]
        x_im = b_scr[r0:r0 + 8, half:]
        for lvl, sh in enumerate((1, 2, 4)):
            a_re = m_ref[0, 16 * lvl:16 * lvl + 8, :]
            a_im = m_ref[0, 16 * lvl + 8:16 * lvl + 16, :]
            r_re = pltpu.roll(x_re, sh, 0)
            r_im = pltpu.roll(x_im, sh, 0)
            x_re, x_im = (x_re + a_re * r_re - a_im * r_im,
                          x_im + a_re * r_im + a_im * r_re)
        p_re = m_ref[0, 48:56, :]
        p_im = m_ref[0, 56:64, :]
        s_re = x_re + p_re * c_re - p_im * c_im
        s_im = x_im + p_re * c_im + p_im * c_re
        s_scr[r0:r0 + 8, :half] = jnp.where(first_row, c_re, pltpu.roll(s_re, 1, 0))
        s_scr[r0:r0 + 8, half:] = jnp.where(first_row, c_im, pltpu.roll(s_im, 1, 0))
        c_re = jnp.broadcast_to(s_re[7:8, :], (8, half))
        c_im = jnp.broadcast_to(s_im[7:8, :], (8, half))
    carry_scr[:, :half] = c_re
    carry_scr[:, half:] = c_im

    y = _dot(x, w_ref[0]) + _dot(s_scr[...].astype(BF16), p_ref[0])
    for t in range(S5_SUB):
        y_ref[pl.ds(t, rows, stride=S5_SUB), :] = y[:, t * LANES:(t + 1) * LANES]


def _s5_core(h, w, q, p, mult, *, tm):
    L = h.shape[0]
    rows = tm // S5_SUB
    wide = S5_SUB * LANES
    return pl.pallas_call(
        functools.partial(_s5_core_kernel, rows=rows),
        grid=(S5_GB, L // tm),
        in_specs=[
            pl.BlockSpec((tm, LANES), lambda g, i: (i, g)),
            pl.BlockSpec((1, wide, wide), lambda g, i: (g, 0, 0)),
            pl.BlockSpec((1, wide, wide), lambda g, i: (g, 0, 0)),
            pl.BlockSpec((1, wide, wide), lambda g, i: (g, 0, 0)),
            pl.BlockSpec((1, 64, S5_GB * S5_STATE), lambda g, i: (g, 0, 0)),
        ],
        out_specs=pl.BlockSpec((tm, LANES), lambda g, i: (i, g)),
        out_shape=jax.ShapeDtypeStruct((L, D_MODEL), F32),
        scratch_shapes=[pltpu.VMEM((8, wide), F32), pltpu.VMEM((rows, wide), F32),
                        pltpu.VMEM((rows, wide), F32)],
        compiler_params=_params("parallel", "arbitrary"),
        name="s5_core",
    )(h, w, q, p, mult)


def _gelu_tanh(x):
    return x * (0.5 * (1.0 + jnp.tanh(math.sqrt(2.0 / math.pi) * (x + 0.044715 * (x * x * x)))))


def _s5_post_kernel(x_ref, h_ref, y_ref, d_ref, w_ref, o_ref):
    y = y_ref[...] + d_ref[...] * h_ref[...]
    z = _gelu_tanh(y).astype(BF16)
    ab = _dot(z, w_ref[...])
    o_ref[...] = x_ref[...] + ab[:, :D_MODEL] * jax.nn.sigmoid(ab[:, D_MODEL:])


def _s5_post(x, h, y, d, w_glu, *, tm):
    L = x.shape[0]
    row = pl.BlockSpec((tm, D_MODEL), lambda i: (i, 0))
    return pl.pallas_call(
        _s5_post_kernel,
        grid=(L // tm,),
        in_specs=[row, row, row,
                  pl.BlockSpec((1, D_MODEL), lambda i: (0, 0)),
                  pl.BlockSpec((D_MODEL, 2 * D_MODEL), lambda i: (0, 0))],
        out_specs=row,
        out_shape=jax.ShapeDtypeStruct((L, D_MODEL), F32),
        compiler_params=_params("parallel"),
        name="s5_post",
    )(x, h, y, d, w_glu)


def _s5_mixer_layer(x, g, lam_re, lam_im, log_dt, b_re, b_im, c_re, c_im, d_skip, w_glu, *, tm, tm_core):
    w, q, p, mult = _s5_tables(lam_re, lam_im, log_dt, b_re, b_im, c_re, c_im)
    h = _rms_layer(x, g, tm=tm)
    y = _s5_core(h, w, q, p, mult, tm=tm_core)
    return _s5_post(x, h, y, d_skip.reshape(1, D_MODEL), w_glu.astype(BF16), tm=tm)


def _gla_kernel(x_ref, g_ref, wq_ref, wk_ref, wv_ref, wg_ref, wa1_ref, wa2_ref, ba_ref, ng_ref,
                wo_ref, tri_ref, same_ref, o_ref, st_scr, o_scr, *, tm):
    @pl.when(pl.program_id(0) == 0)
    def _():
        st_scr[...] = jnp.zeros_like(st_scr)

    x = x_ref[...]
    h = _rms(x, g_ref[...]).astype(BF16)
    q = (_dot(h, wq_ref[...]) * (GLA_DKH ** -0.5)).astype(BF16)
    k = _dot(h, wk_ref[...])
    v = _dot(h, wv_ref[...]).astype(BF16)
    gate = _dot(h, wg_ref[...])
    a_lo = _dot(h, wa1_ref[...]).astype(BF16)
    log_a = jax.nn.log_sigmoid(_dot(a_lo, wa2_ref[...]) + ba_ref[...]) / GLA_TEMP

    la_hi = log_a.astype(BF16)
    la_lo = (log_a - la_hi.astype(F32)).astype(BF16)
    cum = _dot(tri_ref[...], la_hi) + _dot(tri_ref[...], la_lo)
    tot = _dot(same_ref[...], la_hi) + _dot(same_ref[...], la_lo)
    k_dec = (k * jnp.exp(tot - cum)).astype(BF16)

    for c in range(tm // CHUNK):
        r0 = c * CHUNK
        dec_row = jnp.exp(tot[r0:r0 + 1, :])
        for hd in range(GLA_HEADS):
            ks = slice(hd * GLA_DKH, (hd + 1) * GLA_DKH)
            vs = slice(hd * GLA_DVH, (hd + 1) * GLA_DVH)
            upd_t = _dot_tn(v[r0:r0 + CHUNK, vs], k_dec[r0:r0 + CHUNK, ks])
            st = st_scr[hd] * dec_row[:, ks] + upd_t
            st_scr[hd] = st
            o_scr[r0:r0 + CHUNK, vs] = _dot_nt(q[r0:r0 + CHUNK, ks], st.astype(BF16))

    outs = []
    for hd in range(GLA_HEADS):
        vs = slice(hd * GLA_DVH, (hd + 1) * GLA_DVH)
        o = o_scr[:, vs]
        outs.append(o * lax.rsqrt(jnp.mean(o * o, axis=-1, keepdims=True) + NORM_EPS) * ng_ref[:, vs])
    o = jnp.concatenate(outs, axis=1) * (gate * jax.nn.sigmoid(gate))
    o_ref[...] = x + _dot(o.astype(BF16), wo_ref[...])


def _gla_mixer_layer(x, g, w_in, w_a2, b_a, norm_g, w_o, *, tm):
    L = x.shape[0]
    w_in = w_in.astype(BF16)
    wq = w_in[:, :GLA_DK]
    wk = w_in[:, GLA_DK:2 * GLA_DK]
    wv = w_in[:, 2 * GLA_DK:2 * GLA_DK + GLA_DV]
    wg = w_in[:, 2 * GLA_DK + GLA_DV:2 * GLA_DK + 2 * GLA_DV]
    wa1 = jnp.pad(w_in[:, 2 * GLA_DK + 2 * GLA_DV:], ((0, 0), (0, LANES - GLA_GATE_RANK)))
    wa2 = jnp.pad(w_a2.astype(BF16), ((0, LANES - GLA_GATE_RANK), (0, 0)))
    r = jnp.arange(tm)
    same = (r[:, None] // CHUNK) == (r[None, :] // CHUNK)
    tri = (same & (r[None, :] <= r[:, None])).astype(BF16)
    same = same.astype(BF16)

    def full(shape):
        return pl.BlockSpec(shape, lambda i: (0,) * len(shape))

    row = pl.BlockSpec((tm, D_MODEL), lambda i: (i, 0))
    return pl.pallas_call(
        functools.partial(_gla_kernel, tm=tm),
        grid=(L // tm,),
        in_specs=[row, full((1, D_MODEL)), full((D_MODEL, GLA_DK)), full((D_MODEL, GLA_DK)),
                  full((D_MODEL, GLA_DV)), full((D_MODEL, GLA_DV)), full((D_MODEL, LANES)),
                  full((LANES, GLA_DK)), full((1, GLA_DK)), full((1, GLA_DV)),
                  full((GLA_DV, D_MODEL)), full((tm, tm)), full((tm, tm))],
        out_specs=row,
        out_shape=jax.ShapeDtypeStruct((L, D_MODEL), F32),
        scratch_shapes=[pltpu.VMEM((GLA_HEADS, GLA_DVH, GLA_DKH), F32),
                        pltpu.VMEM((tm, GLA_DV), F32)],
        compiler_params=_params("arbitrary"),
        name="gla",
    )(x, g, wq, wk, wv, wg, wa1, wa2, b_a.reshape(1, GLA_DK), norm_g.reshape(1, GLA_DV),
      w_o.astype(BF16), tri, same)


def _rope_tables(positions):
    half = ROT_DIMS // 2
    inv_freq = ROPE_THETA ** (-jnp.arange(half, dtype=F32) / half)
    ang = positions.astype(F32)[:, None] * inv_freq
    cos = jnp.cos(ang)
    sin = jnp.sin(ang)
    m = jnp.arange(LANES) % DIFF_HD
    cos_l = cos[:, m % half]
    sin_l = sin[:, m % half]
    c = jnp.where(m < ROT_DIMS, cos_l, 1.0)
    s_lo = jnp.where(m < half, -sin_l, 0.0)
    s_hi = jnp.where((m >= half) & (m < ROT_DIMS), sin_l, 0.0)
    return c, s_lo, s_hi


def _qkv_kernel(x_ref, g_ref, w_ref, c_ref, slo_ref, shi_ref, q_ref, k_ref, v_ref):
    h = _rms(x_ref[...], g_ref[...]).astype(BF16)
    qkv = _dot(h, w_ref[...])
    c = c_ref[...]
    s_lo = slo_ref[...]
    s_hi = shi_ref[...]
    half = ROT_DIMS // 2

    def rope(t):
        return t * c + pltpu.roll(t, LANES - half, 1) * s_lo + pltpu.roll(t, half, 1) * s_hi

    for j in range(D_MODEL // LANES):
        ls = slice(j * LANES, (j + 1) * LANES)
        q_ref[:, ls] = (rope(qkv[:, ls]) * (DIFF_HD ** -0.5 * LOG2E)).astype(BF16)
        k_ref[:, ls] = rope(qkv[:, D_MODEL + j * LANES:D_MODEL + (j + 1) * LANES]).astype(BF16)
    v_ref[...] = qkv[:, 2 * D_MODEL:].astype(BF16)


def _qkv_layer(x, g, w_qkv, c, s_lo, s_hi, *, tm):
    L = x.shape[0]
    row = pl.BlockSpec((tm, D_MODEL), lambda i: (i, 0))
    tab = pl.BlockSpec((tm, LANES), lambda i: (i, 0))
    out = jax.ShapeDtypeStruct((L, D_MODEL), BF16)
    return pl.pallas_call(
        _qkv_kernel,
        grid=(L // tm,),
        in_specs=[row, pl.BlockSpec((1, D_MODEL), lambda i: (0, 0)),
                  pl.BlockSpec((D_MODEL, 3 * D_MODEL), lambda i: (0, 0)), tab, tab, tab],
        out_specs=[row, row, row],
        out_shape=[out, out, out],
        compiler_params=_params("parallel"),
        name="diff_qkv",
    )(x, g, w_qkv, c, s_lo, s_hi)


def _attn_kernel(qt_ref, k_ref, vt_ref, lamv_ref, sg_ref, o_ref, m_scr, l_scr, acc_scr, *,
                 tq, lambda_init):
    i = pl.program_id(1)
    qt = qt_ref[...]
    row = lax.broadcasted_iota(jnp.int32, qt.shape, 0)
    zero = jnp.zeros_like(qt)
    q_half = (jnp.where(row < DIFF_HD, qt, zero), jnp.where(row >= DIFF_HD, qt, zero))

    m_scr[...] = jnp.full_like(m_scr, NEG_INF)
    l_scr[...] = jnp.zeros_like(l_scr)
    acc_scr[...] = jnp.zeros_like(acc_scr)

    def block(j, masked):
        k = k_ref[0, j]
        vt = vt_ref[0, j]
        if masked:
            kc = lax.broadcasted_iota(jnp.int32, (tq, tq), 0) // CHUNK
            qc = lax.broadcasted_iota(jnp.int32, (tq, tq), 1) // CHUNK
            visible = kc <= qc
        for hf in range(2):
            s = _dot(k, q_half[hf])
            if masked:
                s = jnp.where(visible, s, NEG_INF)
            m_prev = m_scr[hf]
            m_new = jnp.maximum(m_prev, jnp.max(s, axis=0, keepdims=True))
            alpha = jnp.exp2(m_prev - m_new)
            p = jnp.exp2(s - m_new)
            l_scr[hf] = alpha * l_scr[hf] + jnp.sum(p, axis=0, keepdims=True)
            acc_scr[hf] = alpha * acc_scr[hf] + _dot(vt, p.astype(BF16))
            m_scr[hf] = m_new

    def body(j, carry):
        block(j, False)
        return carry

    lax.fori_loop(0, i, body, 0)
    block(i, True)

    lv = lamv_ref[...]
    lam = (jnp.exp(jnp.sum(lv[0:1] * lv[1:2], axis=-1, keepdims=True))
           - jnp.exp(jnp.sum(lv[2:3] * lv[3:4], axis=-1, keepdims=True)) + lambda_init)
    o = acc_scr[0] / l_scr[0] - lam * (acc_scr[1] / l_scr[1])
    o = o * lax.rsqrt(jnp.mean(o * o, axis=0, keepdims=True) + SUBLN_EPS) * sg_ref[...]
    o_ref[...] = (o * (1.0 - lambda_init)).astype(BF16)


def _attn_layer(qt, k, vt, lamv, subln_g, *, tq, lambda_init):
    n = k.shape[1]
    L = n * tq
    return pl.pallas_call(
        functools.partial(_attn_kernel, tq=tq, lambda_init=lambda_init),
        grid=(DIFF_HEADS, n),
        in_specs=[
            pl.BlockSpec((LANES, tq), lambda p, i: (p, i)),
            pl.BlockSpec((1, n, tq, LANES), lambda p, i: (p, 0, 0, 0)),
            pl.BlockSpec((1, n, LANES, tq), lambda p, i: (p, 0, 0, 0)),
            pl.BlockSpec((4, DIFF_HD), lambda p, i: (0, 0)),
            pl.BlockSpec((LANES, 1), lambda p, i: (0, 0)),
        ],
        out_specs=pl.BlockSpec((LANES, tq), lambda p, i: (p, i)),
        out_shape=jax.ShapeDtypeStruct((D_MODEL, L), BF16),
        scratch_shapes=[pltpu.VMEM((2, 1, tq), F32), pltpu.VMEM((2, 1, tq), F32),
                        pltpu.VMEM((2, LANES, tq), F32)],
        compiler_params=_params("parallel", "arbitrary"),
        name="diff_attn",
    )(qt, k, vt, lamv, subln_g)


def _proj_kernel(x_ref, at_ref, w_ref, o_ref):
    o_ref[...] = x_ref[...] + _dot_tn(at_ref[...], w_ref[...])


def _proj_layer(x, at, w, *, tm):
    L = x.shape[0]
    row = pl.BlockSpec((tm, D_MODEL), lambda i: (i, 0))
    return pl.pallas_call(
        _proj_kernel,
        grid=(L // tm,),
        in_specs=[row, pl.BlockSpec((D_MODEL, tm), lambda i: (0, i)),
                  pl.BlockSpec((D_MODEL, D_MODEL), lambda i: (0, 0))],
        out_specs=row,
        out_shape=jax.ShapeDtypeStruct((L, D_MODEL), F32),
        compiler_params=_params("parallel"),
        name="diff_out",
    )(x, at, w)


def _diff_mixer_layer(x, g, positions, w_qkv, lam_q1, lam_k1, lam_q2, lam_k2, subln_g, w_o,
                      lambda_init, *, tm, tq):
    L = x.shape[0]
    n = L // tq
    c, s_lo, s_hi = _rope_tables(positions)
    q, k, v = _qkv_layer(x, g, w_qkv.astype(BF16), c, s_lo, s_hi, tm=tm)
    qt = q.T
    k4 = k.reshape(n, tq, DIFF_HEADS, LANES).transpose(2, 0, 1, 3)
    vt4 = v.reshape(n, tq, DIFF_HEADS, LANES).transpose(2, 0, 3, 1)
    lamv = jnp.stack([lam_q1, lam_k1, lam_q2, lam_k2]).astype(F32)
    at = _attn_layer(qt, k4, vt4, lamv, subln_g.reshape(LANES, 1).astype(F32), tq=tq,
                     lambda_init=lambda_init)
    return _proj_layer(x, at, w_o.astype(BF16), tm=tm)


def _trunk(x, positions, norm_mix, norm_ffn, norm_final,
           s5_lam_re, s5_lam_im, s5_log_dt, s5_b_re, s5_b_im, s5_c_re, s5_c_im, s5_d, s5_w_glu,
           gla_w_in, gla_w_a2, gla_b_a, gla_norm, gla_w_o,
           diff_w_qkv, diff_lam_q1, diff_lam_k1, diff_lam_q2, diff_lam_k2, diff_subln, diff_w_o,
           ffn_w_gate_up, ffn_w_down, *, tm, tm_core, tm_gla, tq, tm_ffn, th):
    depth = norm_mix.shape[0]
    gfinal = norm_final.reshape(1, D_MODEL)
    for layer in range(depth):
        kind = layer % N_MIXERS
        idx = layer // N_MIXERS
        g = norm_mix[layer].reshape(1, D_MODEL)
        if kind == 0:
            x = _s5_mixer_layer(x, g, s5_lam_re[idx], s5_lam_im[idx], s5_log_dt[idx], s5_b_re[idx],
                                s5_b_im[idx], s5_c_re[idx], s5_c_im[idx], s5_d[idx], s5_w_glu[idx],
                                tm=tm, tm_core=tm_core)
        elif kind == 1:
            x = _gla_mixer_layer(x, g, gla_w_in[idx], gla_w_a2[idx], gla_b_a[idx], gla_norm[idx],
                                 gla_w_o[idx], tm=tm_gla)
        else:
            lambda_init = 0.8 - 0.6 * math.exp(-0.3 * layer)
            x = _diff_mixer_layer(x, g, positions, diff_w_qkv[idx], diff_lam_q1[idx],
                                  diff_lam_k1[idx], diff_lam_q2[idx], diff_lam_k2[idx],
                                  diff_subln[idx], diff_w_o[idx], lambda_init, tm=tm, tq=tq)
        x = _ffn_layer(x, norm_ffn[layer].reshape(1, D_MODEL), ffn_w_gate_up[layer].astype(BF16),
                       ffn_w_down[layer].astype(BF16), gfinal, final=(layer == depth - 1),
                       tm=tm_ffn, th=th)
    return x


def kernel(x, positions, norm_mix, norm_ffn, norm_final, s5_lam_re, s5_lam_im, s5_log_dt, s5_b_re, s5_b_im, s5_c_re, s5_c_im, s5_d, s5_w_glu, gla_w_in, gla_w_a2, gla_b_a, gla_norm, gla_w_o, diff_w_qkv, diff_lam_q1, diff_lam_k1, diff_lam_q2, diff_lam_k2, diff_subln, diff_w_o, ffn_w_gate_up, ffn_w_down):
    bsz, seq, _ = x.shape
    outs = []
    for b in range(bsz):
        outs.append(_trunk(
            x[b], positions[b], norm_mix, norm_ffn, norm_final,
            s5_lam_re, s5_lam_im, s5_log_dt, s5_b_re, s5_b_im, s5_c_re, s5_c_im, s5_d, s5_w_glu,
            gla_w_in, gla_w_a2, gla_b_a, gla_norm, gla_w_o,
            diff_w_qkv, diff_lam_q1, diff_lam_k1, diff_lam_q2, diff_lam_k2, diff_subln, diff_w_o,
            ffn_w_gate_up, ffn_w_down,
            tm=512, tm_core=4096, tm_gla=512, tq=512, tm_ffn=1024, th=256))
    return jnp.stack(outs)
```

```python
import functools
import math

import jax
import jax.numpy as jnp
from jax import lax
from jax.experimental import pallas as pl
from jax.experimental.pallas import tpu as pltpu

F32 = jnp.float32
BF16 = jnp.bfloat16

D_MODEL = 1024
DEPTH = 4
CHUNK = 64
N_MIXERS = 3
NORM_EPS = 1e-6
S5_GROUP = 16
S5_GROUPS = D_MODEL // S5_GROUP
S5_STATE = 64
S5_SUB = 8
S5_GB = 8
GLA_HEADS = 4
GLA_DK = D_MODEL // 2
GLA_DV = D_MODEL
GLA_DKH = GLA_DK // GLA_HEADS
GLA_DVH = GLA_DV // GLA_HEADS
GLA_GATE_RANK = 16
GLA_TEMP = 16.0
DIFF_HD = 64
DIFF_HEADS = D_MODEL // (2 * DIFF_HD)
ROT_DIMS = DIFF_HD // 4
ROPE_THETA = 500000.0
NEG_INF = -1e30
SUBLN_EPS = 1e-5
LOG2E = math.log2(math.e)
FFN_HIDDEN = -(-8 * D_MODEL // (3 * 256)) * 256

LANES = 128
VMEM_LIMIT_BYTES = 56 * 1024 * 1024


def _params(*sem):
    return pltpu.CompilerParams(dimension_semantics=sem, vmem_limit_bytes=VMEM_LIMIT_BYTES)


def _rms(x, g, eps=NORM_EPS):
    return x * lax.rsqrt(jnp.mean(x * x, axis=-1, keepdims=True) + eps) * g


def _dot(a, b):
    return jnp.dot(a, b, preferred_element_type=F32)


def _dot_nt(a, b):
    return lax.dot_general(a, b, (((1,), (1,)), ((), ())), preferred_element_type=F32)


def _dot_tn(a, b):
    return lax.dot_general(a, b, (((0,), (0,)), ((), ())), preferred_element_type=F32)


def _ffn_kernel(x_ref, g_ref, wgu_ref, wd_ref, gf_ref, o_ref, h_scr, *, nk, th, final):
    x = x_ref[...]
    h_scr[...] = _rms(x, g_ref[...]).astype(BF16)
    o_ref[...] = x
    for c in range(nk):
        gu = _dot(h_scr[...], wgu_ref[c])
        gate = gu[:, :th]
        act = (gate * jax.nn.sigmoid(gate) * gu[:, th:]).astype(BF16)
        o_ref[...] += _dot(act, wd_ref[c])
    if final:
        o_ref[...] = _rms(o_ref[...], gf_ref[...])


def _resident(shape):
    return pl.BlockSpec(shape, lambda i: (0,) * len(shape), pipeline_mode=pl.Buffered(1))


def _ffn_layer(x, g, w_gate_up, w_down, gf, *, final, tm, th):
    L = x.shape[0]
    nk = FFN_HIDDEN // th
    wgu = w_gate_up.astype(BF16).reshape(D_MODEL, 2, nk, th).transpose(2, 0, 1, 3)
    wgu = wgu.reshape(nk, D_MODEL, 2 * th)
    wd = w_down.astype(BF16).reshape(nk, th, D_MODEL)
    return pl.pallas_call(
        functools.partial(_ffn_kernel, nk=nk, th=th, final=final),
        grid=(L // tm,),
        in_specs=[
            pl.BlockSpec((tm, D_MODEL), lambda i: (i, 0)),
            pl.BlockSpec((1, D_MODEL), lambda i: (0, 0)),
            _resident((nk, D_MODEL, 2 * th)),
            _resident((nk, th, D_MODEL)),
            pl.BlockSpec((1, D_MODEL), lambda i: (0, 0)),
        ],
        out_specs=pl.BlockSpec((tm, D_MODEL), lambda i: (i, 0)),
        out_shape=jax.ShapeDtypeStruct((L, D_MODEL), F32),
        scratch_shapes=[pltpu.VMEM((tm, D_MODEL), BF16)],
        compiler_params=_params("parallel"),
        name="ffn",
    )(x, g, wgu, wd, gf)


def _s5_tables(lam_re, lam_im, log_dt, b_re, b_im, c_re, c_im):
    G, P, H, T, NB = S5_GROUPS, S5_STATE, S5_GROUP, S5_SUB, S5_GB
    f32 = F32
    lr = lam_re.astype(f32)
    li = lam_im.astype(f32)
    dt = jnp.exp(log_dt.astype(f32))[:, None]
    ab_mag = jnp.exp(lr * dt)
    ab_ang = li * dt
    ab_re = ab_mag * jnp.cos(ab_ang)
    ab_im = ab_mag * jnp.sin(ab_ang)
    den = lr * lr + li * li
    f_re = ((ab_re - 1.0) * lr + ab_im * li) / den
    f_im = (ab_im * lr - (ab_re - 1.0) * li) / den
    br = b_re.astype(f32)
    bi = b_im.astype(f32)
    bb_re = f_re[..., None] * br - f_im[..., None] * bi
    bb_im = f_re[..., None] * bi + f_im[..., None] * br
    cr = c_re.astype(f32)
    ci = c_im.astype(f32)

    def powers(n):
        n = n.astype(f32)[:, None, None]
        mag = jnp.exp(lr * dt * n)
        ang = li * dt * n
        return mag * jnp.cos(ang), mag * jnp.sin(ang)

    pr, pi = powers(jnp.arange(T + 1))
    cp_re = cr[None] * pr[:, :, None, :] - ci[None] * pi[:, :, None, :]
    cp_im = cr[None] * pi[:, :, None, :] + ci[None] * pr[:, :, None, :]
    hp = lax.Precision.HIGHEST
    kern = (jnp.einsum("ngkp,gph->ngkh", cp_re[:T], bb_re, precision=hp)
            - jnp.einsum("ngkp,gph->ngkh", cp_im[:T], bb_im, precision=hp))


    def blockdiag(xc, rows_per, cols_per):
        r = jnp.arange(NB * rows_per)[:, None, None] // rows_per
        c = jnp.arange(NB * cols_per)[None, None, :] // cols_per
        tiled = jnp.tile(xc, (1, 1, 1, 1, NB))
        out = jnp.where(r == c, tiled, 0.0).astype(BF16)
        return out.reshape(NB, xc.shape[1] * NB * rows_per, xc.shape[3] * NB * cols_per)

    t_in = jnp.arange(T)[:, None]
    t_out = jnp.arange(T)[None, :]
    lag = jnp.clip(t_out - t_in, 0, T - 1)
    causal = (t_out >= t_in).astype(f32)
    xk = kern.transpose(1, 0, 3, 2).reshape(NB, NB, T, H, H)
    xk = xk.transpose(0, 2, 1, 3, 4).reshape(NB, T, NB * H, H)
    xk = xk[:, lag] * causal[None, :, :, None, None]
    w = blockdiag(xk.transpose(0, 1, 3, 2, 4), H, H)

    prq = pr[T - 1 - jnp.arange(T)]
    piq = pi[T - 1 - jnp.arange(T)]
    qv_re = prq[..., None] * bb_re[None] - piq[..., None] * bb_im[None]
    qv_im = prq[..., None] * bb_im[None] + piq[..., None] * bb_re[None]
    qv = jnp.stack([qv_re, qv_im]).reshape(2, T, NB, NB, P, H)
    qv = qv.transpose(2, 1, 3, 5, 0, 4).reshape(NB, T, NB * H, 2, P)
    q = blockdiag(qv, H, P)

    pm = jnp.stack([cp_re[1:], -cp_im[1:]]).reshape(2, T, NB, NB, H, P)
    pm = pm.transpose(2, 0, 3, 5, 1, 4).reshape(NB, 2, NB * P, T, H)
    p = blockdiag(pm, P, H)

    sr, si = powers(T * jnp.arange(1, 9))
    rows = jnp.arange(8)
    kinds = []
    for sh in (1, 2, 4):
        m = (rows >= sh).astype(f32)[:, None, None]
        kinds += [m * sr[sh - 1][None], m * si[sh - 1][None]]
    kinds += [sr, si]
    mult = jnp.stack(kinds)
    mult = mult.reshape(8, 8, NB, NB * P).transpose(2, 0, 1, 3).reshape(NB, 64, NB * P)
    return w, q, p, mult


def _rms_kernel(x_ref, g_ref, o_ref):
    o_ref[...] = _rms(x_ref[...], g_ref[...])


def _rms_layer(x, g, *, tm):
    L = x.shape[0]
    return pl.pallas_call(
        _rms_kernel,
        grid=(L // tm,),
        in_specs=[pl.BlockSpec((tm, D_MODEL), lambda i: (i, 0)),
                  pl.BlockSpec((1, D_MODEL), lambda i: (0, 0))],
        out_specs=pl.BlockSpec((tm, D_MODEL), lambda i: (i, 0)),
        out_shape=jax.ShapeDtypeStruct((L, D_MODEL), F32),
        compiler_params=_params("parallel"),
        name="s5_norm",
    )(x, g)


def _s5_core_kernel(h_ref, w_ref, q_ref, p_ref, m_ref, y_ref, carry_scr, b_scr, s_scr, *, rows):
    half = S5_GB * S5_STATE

    @pl.when(pl.program_id(1) == 0)
    def _():
        carry_scr[...] = jnp.zeros_like(carry_scr)

    x = jnp.concatenate(
        [h_ref[pl.ds(t, rows, stride=S5_SUB), :] for t in range(S5_SUB)], axis=1).astype(BF16)
    b_scr[...] = _dot(x, q_ref[0])

    first_row = lax.broadcasted_iota(jnp.int32, (8, half), 0) == 0
    c_re = carry_scr[:, :half]
    c_im = carry_scr[:, half:]
    for tile in range(rows // 8):
        r0 = tile * 8
        x_re = b_scr[r0:r0 + 8, :half]
        x_im = b_scr[r0:r0 + 8, half:]
        for lvl, sh in enumerate((1, 2, 4)):
            a_re = m_ref[0, 16 * lvl:16 * lvl + 8, :]
            a_im = m_ref[0, 16 * lvl + 8:16 * lvl + 16, :]
            r_re = pltpu.roll(x_re, sh, 0)
            r_im = pltpu.roll(x_im, sh, 0)
            x_re, x_im = (x_re + a_re * r_re - a_im * r_im,
                          x_im + a_re * r_im + a_im * r_re)
        p_re = m_ref[0, 48:56, :]
        p_im = m_ref[0, 56:64, :]
        s_re = x_re + p_re * c_re - p_im * c_im
        s_im = x_im + p_re * c_im + p_im * c_re
        s_scr[r0:r0 + 8, :half] = jnp.where(first_row, c_re, pltpu.roll(s_re, 1, 0))
        s_scr[r0:r0 + 8, half:] = jnp.where(first_row, c_im, pltpu.roll(s_im, 1, 0))
        c_re = jnp.broadcast_to(s_re[7:8, :], (8, half))
        c_im = jnp.broadcast_to(s_im[7:8, :], (8, half))
    carry_scr[:, :half] = c_re
    carry_scr[:, half:] = c_im

    y = _dot(x, w_ref[0]) + _dot(s_scr[...].astype(BF16), p_ref[0])
    for t in range(S5_SUB):
        y_ref[pl.ds(t, rows, stride=S5_SUB), :] = y[:, t * LANES:(t + 1) * LANES]


def _s5_core(h, w, q, p, mult, *, tm):
    L = h.shape[0]
    rows = tm // S5_SUB
    wide = S5_SUB * LANES
    return pl.pallas_call(
        functools.partial(_s5_core_kernel, rows=rows),
        grid=(S5_GB, L // tm),
        in_specs=[
            pl.BlockSpec((tm, LANES), lambda g, i: (i, g)),
            pl.BlockSpec((1, wide, wide), lambda g, i: (g, 0, 0)),
            pl.BlockSpec((1, wide, wide), lambda g, i: (g, 0, 0)),
            pl.BlockSpec((1, wide, wide), lambda g, i: (g, 0, 0)),
            pl.BlockSpec((1, 64, S5_GB * S5_STATE), lambda g, i: (g, 0, 0)),
        ],
        out_specs=pl.BlockSpec((tm, LANES), lambda g, i: (i, g)),
        out_shape=jax.ShapeDtypeStruct((L, D_MODEL), F32),
        scratch_shapes=[pltpu.VMEM((8, wide), F32), pltpu.VMEM((rows, wide), F32),
                        pltpu.VMEM((rows, wide), F32)],
        compiler_params=_params("parallel", "arbitrary"),
        name="s5_core",
    )(h, w, q, p, mult)


def _gelu_tanh(x):
    return x * (0.5 * (1.0 + jnp.tanh(math.sqrt(2.0 / math.pi) * (x + 0.044715 * (x * x * x)))))


def _s5_post_kernel(x_ref, h_ref, y_ref, d_ref, w_ref, o_ref):
    y = y_ref[...] + d_ref[...] * h_ref[...]
    z = _gelu_tanh(y).astype(BF16)
    ab = _dot(z, w_ref[...])
    o_ref[...] = x_ref[...] + ab[:, :D_MODEL] * jax.nn.sigmoid(ab[:, D_MODEL:])


def _s5_post(x, h, y, d, w_glu, *, tm):
    L = x.shape[0]
    row = pl.BlockSpec((tm, D_MODEL), lambda i: (i, 0))
    return pl.pallas_call(
        _s5_post_kernel,
        grid=(L // tm,),
        in_specs=[row, row, row,
                  pl.BlockSpec((1, D_MODEL), lambda i: (0, 0)),
                  pl.BlockSpec((D_MODEL, 2 * D_MODEL), lambda i: (0, 0))],
        out_specs=row,
        out_shape=jax.ShapeDtypeStruct((L, D_MODEL), F32),
        compiler_params=_params("parallel"),
        name="s5_post",
    )(x, h, y, d, w_glu)


def _s5_mixer_layer(x, g, lam_re, lam_im, log_dt, b_re, b_im, c_re, c_im, d_skip, w_glu, *, tm, tm_core):
    w, q, p, mult = _s5_tables(lam_re, lam_im, log_dt, b_re, b_im, c_re, c_im)
    h = _rms_layer(x, g, tm=tm)
    y = _s5_core(h, w, q, p, mult, tm=tm_core)
    return _s5_post(x, h, y, d_skip.reshape(1, D_MODEL), w_glu.astype(BF16), tm=tm)


def _gla_kernel(x_ref, g_ref, wq_ref, wk_ref, wv_ref, wg_ref, wa1_ref, wa2_ref, ba_ref, ng_ref,
                wo_ref, tri_ref, o_ref, st_scr, o_scr, upd_scr, stb_scr, *, tm):
    @pl.when(pl.program_id(0) == 0)
    def _():
        st_scr[...] = jnp.zeros_like(st_scr)

    x = x_ref[...]
    h = _rms(x, g_ref[...]).astype(BF16)
    q = (_dot(h, wq_ref[...]) * (GLA_DKH ** -0.5)).astype(BF16)
    k = _dot(h, wk_ref[...])
    v = _dot(h, wv_ref[...]).astype(BF16)
    gate = _dot(h, wg_ref[...])
    a_lo = _dot(h, wa1_ref[...]).astype(BF16)
    log_a = jax.nn.log_sigmoid(_dot(a_lo, wa2_ref[...]) + ba_ref[...]) / GLA_TEMP

    la_hi = log_a.astype(BF16)
    la_lo = (log_a - la_hi.astype(F32)).astype(BF16)
    cum = _dot(tri_ref[...], la_hi) + _dot(tri_ref[...], la_lo)
    n_chunks = tm // CHUNK
    tot_rows = [cum[(c + 1) * CHUNK - 1:(c + 1) * CHUNK, :] for c in range(n_chunks)]
    tot = jnp.concatenate([jnp.broadcast_to(t, (CHUNK, GLA_DK)) for t in tot_rows], axis=0)
    k_dec = (k * jnp.exp(tot - cum)).astype(BF16)

    heads = [(slice(hd * GLA_DKH, (hd + 1) * GLA_DKH), slice(hd * GLA_DVH, (hd + 1) * GLA_DVH))
             for hd in range(GLA_HEADS)]
    for c in range(n_chunks):
        rs = slice(c * CHUNK, (c + 1) * CHUNK)
        for hd, (ks, vs) in enumerate(heads):
            upd_scr[c, hd] = _dot_tn(v[rs, vs], k_dec[rs, ks])
    for c in range(n_chunks):
        dec_row = jnp.exp(tot_rows[c])
        for hd, (ks, vs) in enumerate(heads):
            st = st_scr[hd] * dec_row[:, ks] + upd_scr[c, hd]
            st_scr[hd] = st
            stb_scr[c, hd] = st.astype(BF16)
    for c in range(n_chunks):
        rs = slice(c * CHUNK, (c + 1) * CHUNK)
        for hd, (ks, vs) in enumerate(heads):
            o_scr[rs, vs] = _dot_nt(q[rs, ks], stb_scr[c, hd])

    outs = []
    for hd in range(GLA_HEADS):
        vs = slice(hd * GLA_DVH, (hd + 1) * GLA_DVH)
        o = o_scr[:, vs]
        outs.append(o * lax.rsqrt(jnp.mean(o * o, axis=-1, keepdims=True) + NORM_EPS) * ng_ref[:, vs])
    o = jnp.concatenate(outs, axis=1) * (gate * jax.nn.sigmoid(gate))
    o_ref[...] = x + _dot(o.astype(BF16), wo_ref[...])


def _gla_mixer_layer(x, g, w_in, w_a2, b_a, norm_g, w_o, *, tm):
    L = x.shape[0]
    w_in = w_in.astype(BF16)
    wq = w_in[:, :GLA_DK]
    wk = w_in[:, GLA_DK:2 * GLA_DK]
    wv = w_in[:, 2 * GLA_DK:2 * GLA_DK + GLA_DV]
    wg = w_in[:, 2 * GLA_DK + GLA_DV:2 * GLA_DK + 2 * GLA_DV]
    wa1 = jnp.pad(w_in[:, 2 * GLA_DK + 2 * GLA_DV:], ((0, 0), (0, LANES - GLA_GATE_RANK)))
    wa2 = jnp.pad(w_a2.astype(BF16), ((0, LANES - GLA_GATE_RANK), (0, 0)))
    r = jnp.arange(tm)
    tri = (((r[:, None] // CHUNK) == (r[None, :] // CHUNK)) & (r[None, :] <= r[:, None])).astype(BF16)
    full = _resident
    n_chunks = tm // CHUNK

    row = pl.BlockSpec((tm, D_MODEL), lambda i: (i, 0))
    return pl.pallas_call(
        functools.partial(_gla_kernel, tm=tm),
        grid=(L // tm,),
        in_specs=[row, full((1, D_MODEL)), full((D_MODEL, GLA_DK)), full((D_MODEL, GLA_DK)),
                  full((D_MODEL, GLA_DV)), full((D_MODEL, GLA_DV)), full((D_MODEL, LANES)),
                  full((LANES, GLA_DK)), full((1, GLA_DK)), full((1, GLA_DV)),
                  full((GLA_DV, D_MODEL)), full((tm, tm))],
        out_specs=row,
        out_shape=jax.ShapeDtypeStruct((L, D_MODEL), F32),
        scratch_shapes=[pltpu.VMEM((GLA_HEADS, GLA_DVH, GLA_DKH), F32),
                        pltpu.VMEM((tm, GLA_DV), F32),
                        pltpu.VMEM((n_chunks, GLA_HEADS, GLA_DVH, GLA_DKH), F32),
                        pltpu.VMEM((n_chunks, GLA_HEADS, GLA_DVH, GLA_DKH), BF16)],
        compiler_params=_params("arbitrary"),
        name="gla",
    )(x, g, wq, wk, wv, wg, wa1, wa2, b_a.reshape(1, GLA_DK), norm_g.reshape(1, GLA_DV),
      w_o.astype(BF16), tri)


def _rope_tables(positions):
    half = ROT_DIMS // 2
    inv_freq = ROPE_THETA ** (-jnp.arange(half, dtype=F32) / half)
    ang = positions.astype(F32)[:, None] * inv_freq
    cos = jnp.cos(ang)
    sin = jnp.sin(ang)
    m = jnp.arange(LANES) % DIFF_HD
    cos_l = cos[:, m % half]
    sin_l = sin[:, m % half]
    c = jnp.where(m < ROT_DIMS, cos_l, 1.0)
    s_lo = jnp.where(m < half, -sin_l, 0.0)
    s_hi = jnp.where((m >= half) & (m < ROT_DIMS), sin_l, 0.0)
    return c, s_lo, s_hi


def _qkv_kernel(x_ref, g_ref, w_ref, c_ref, slo_ref, shi_ref, qt_ref, k_ref, vt_ref):
    h = _rms(x_ref[...], g_ref[...]).astype(BF16)
    qkv = _dot(h, w_ref[...])
    c = c_ref[...]
    s_lo = slo_ref[...]
    s_hi = shi_ref[...]
    half = ROT_DIMS // 2

    def rope(t):
        return t * c + pltpu.roll(t, LANES - half, 1) * s_lo + pltpu.roll(t, half, 1) * s_hi

    for j in range(D_MODEL // LANES):
        ls = slice(j * LANES, (j + 1) * LANES)
        q = rope(qkv[:, ls]) * (DIFF_HD ** -0.5 * LOG2E)
        qt_ref[ls, :] = q.T.astype(BF16)
        k_ref[j, 0] = rope(qkv[:, D_MODEL + j * LANES:D_MODEL + (j + 1) * LANES]).astype(BF16)
        vt_ref[j, 0] = qkv[:, 2 * D_MODEL + j * LANES:2 * D_MODEL + (j + 1) * LANES].T.astype(BF16)


def _qkv_layer(x, g, w_qkv, c, s_lo, s_hi, *, tm):
    L = x.shape[0]
    n = L // tm
    row = pl.BlockSpec((tm, D_MODEL), lambda i: (i, 0))
    tab = pl.BlockSpec((tm, LANES), lambda i: (i, 0))
    return pl.pallas_call(
        _qkv_kernel,
        grid=(n,),
        in_specs=[row, pl.BlockSpec((1, D_MODEL), lambda i: (0, 0)),
                  _resident((D_MODEL, 3 * D_MODEL)), tab, tab, tab],
        out_specs=[pl.BlockSpec((D_MODEL, tm), lambda i: (0, i)),
                   pl.BlockSpec((DIFF_HEADS, 1, tm, LANES), lambda i: (0, i, 0, 0)),
                   pl.BlockSpec((DIFF_HEADS, 1, LANES, tm), lambda i: (0, i, 0, 0))],
        out_shape=[jax.ShapeDtypeStruct((D_MODEL, L), BF16),
                   jax.ShapeDtypeStruct((DIFF_HEADS, n, tm, LANES), BF16),
                   jax.ShapeDtypeStruct((DIFF_HEADS, n, LANES, tm), BF16)],
        compiler_params=_params("parallel"),
        name="diff_qkv",
    )(x, g, w_qkv, c, s_lo, s_hi)


def _attn_kernel(qt_ref, k_ref, vt_ref, lamv_ref, sg_ref, o_ref, m_scr, l_scr, acc_scr,
                 sa_scr, sb_scr, *,
                 tq, lambda_init):
    i = pl.program_id(1)
    qt = qt_ref[...]
    row = lax.broadcasted_iota(jnp.int32, qt.shape, 0)
    zero = jnp.zeros_like(qt)
    q_half = (jnp.where(row < DIFF_HD, qt, zero), jnp.where(row >= DIFF_HD, qt, zero))

    m_scr[...] = jnp.full_like(m_scr, NEG_INF)
    l_scr[...] = jnp.zeros_like(l_scr)
    acc_scr[...] = jnp.zeros_like(acc_scr)

    def scores(j, s_ref):
        k = k_ref[0, j]
        for hf in range(2):
            s_ref[hf] = _dot(k, q_half[hf])

    def update(j, s_ref, masked):
        vt = vt_ref[0, j]
        if masked:
            kc = lax.broadcasted_iota(jnp.int32, (tq, tq), 0) // CHUNK
            qc = lax.broadcasted_iota(jnp.int32, (tq, tq), 1) // CHUNK
            visible = kc <= qc
        for hf in range(2):
            s = s_ref[hf]
            if masked:
                s = jnp.where(visible, s, NEG_INF)
            m_prev = m_scr[hf]
            m_new = jnp.maximum(m_prev, jnp.max(s, axis=0, keepdims=True))
            alpha = jnp.exp2(m_prev - m_new)
            p = jnp.exp2(s - m_new)
            l_scr[hf] = alpha * l_scr[hf] + jnp.sum(p, axis=0, keepdims=True)
            acc_scr[hf] = alpha * acc_scr[hf] + _dot(vt, p.astype(BF16))
            m_scr[hf] = m_new

    scores(0, sa_scr)

    def pair(t, carry):
        j = 2 * t
        scores(j + 1, sb_scr)
        update(j, sa_scr, False)
        scores(j + 2, sa_scr)
        update(j + 1, sb_scr, False)
        return carry

    lax.fori_loop(0, i // 2, pair, 0)

    @pl.when(i % 2 == 0)
    def _():
        update(i, sa_scr, True)

    @pl.when(i % 2 == 1)
    def _():
        scores(i, sb_scr)
        update(i - 1, sa_scr, False)
        update(i, sb_scr, True)

    lv = lamv_ref[...]
    lam = (jnp.exp(jnp.sum(lv[0:1] * lv[1:2], axis=-1, keepdims=True))
           - jnp.exp(jnp.sum(lv[2:3] * lv[3:4], axis=-1, keepdims=True)) + lambda_init)
    o = acc_scr[0] / l_scr[0] - lam * (acc_scr[1] / l_scr[1])
    o = o * lax.rsqrt(jnp.mean(o * o, axis=0, keepdims=True) + SUBLN_EPS) * sg_ref[...]
    o_ref[...] = (o * (1.0 - lambda_init)).astype(BF16)


def _attn_layer(qt, k, vt, lamv, subln_g, *, tq, lambda_init):
    n = k.shape[1]
    L = n * tq
    return pl.pallas_call(
        functools.partial(_attn_kernel, tq=tq, lambda_init=lambda_init),
        grid=(DIFF_HEADS, n),
        in_specs=[
            pl.BlockSpec((LANES, tq), lambda p, i: (p, i)),
            pl.BlockSpec((1, n, tq, LANES), lambda p, i: (p, 0, 0, 0)),
            pl.BlockSpec((1, n, LANES, tq), lambda p, i: (p, 0, 0, 0)),
            pl.BlockSpec((4, DIFF_HD), lambda p, i: (0, 0)),
            pl.BlockSpec((LANES, 1), lambda p, i: (0, 0)),
        ],
        out_specs=pl.BlockSpec((LANES, tq), lambda p, i: (p, i)),
        out_shape=jax.ShapeDtypeStruct((D_MODEL, L), BF16),
        scratch_shapes=[pltpu.VMEM((2, 1, tq), F32), pltpu.VMEM((2, 1, tq), F32),
                        pltpu.VMEM((2, LANES, tq), F32),
                        pltpu.VMEM((2, tq, tq), F32), pltpu.VMEM((2, tq, tq), F32)],
        compiler_params=_params("parallel", "arbitrary"),
        name="diff_attn",
    )(qt, k, vt, lamv, subln_g)


def _proj_kernel(x_ref, at_ref, w_ref, o_ref):
    o_ref[...] = x_ref[...] + _dot_tn(at_ref[...], w_ref[...])


def _proj_layer(x, at, w, *, tm):
    L = x.shape[0]
    row = pl.BlockSpec((tm, D_MODEL), lambda i: (i, 0))
    return pl.pallas_call(
        _proj_kernel,
        grid=(L // tm,),
        in_specs=[row, pl.BlockSpec((D_MODEL, tm), lambda i: (0, i)),
                  pl.BlockSpec((D_MODEL, D_MODEL), lambda i: (0, 0))],
        out_specs=row,
        out_shape=jax.ShapeDtypeStruct((L, D_MODEL), F32),
        compiler_params=_params("parallel"),
        name="diff_out",
    )(x, at, w)


def _diff_mixer_layer(x, g, positions, w_qkv, lam_q1, lam_k1, lam_q2, lam_k2, subln_g, w_o,
                      lambda_init, *, tm, tq):
    c, s_lo, s_hi = _rope_tables(positions)
    qt, k4, vt4 = _qkv_layer(x, g, w_qkv.astype(BF16), c, s_lo, s_hi, tm=tq)
    lamv = jnp.stack([lam_q1, lam_k1, lam_q2, lam_k2]).astype(F32)
    at = _attn_layer(qt, k4, vt4, lamv, subln_g.reshape(LANES, 1).astype(F32), tq=tq,
                     lambda_init=lambda_init)
    return _proj_layer(x, at, w_o.astype(BF16), tm=tm)


def _trunk(x, positions, norm_mix, norm_ffn, norm_final,
           s5_lam_re, s5_lam_im, s5_log_dt, s5_b_re, s5_b_im, s5_c_re, s5_c_im, s5_d, s5_w_glu,
           gla_w_in, gla_w_a2, gla_b_a, gla_norm, gla_w_o,
           diff_w_qkv, diff_lam_q1, diff_lam_k1, diff_lam_q2, diff_lam_k2, diff_subln, diff_w_o,
           ffn_w_gate_up, ffn_w_down, *, tm, tm_core, tm_gla, tq, tm_ffn, th):
    depth = norm_mix.shape[0]
    gfinal = norm_final.reshape(1, D_MODEL)
    for layer in range(depth):
        kind = layer % N_MIXERS
        idx = layer // N_MIXERS
        g = norm_mix[layer].reshape(1, D_MODEL)
        if kind == 0:
            x = _s5_mixer_layer(x, g, s5_lam_re[idx], s5_lam_im[idx], s5_log_dt[idx], s5_b_re[idx],
                                s5_b_im[idx], s5_c_re[idx], s5_c_im[idx], s5_d[idx], s5_w_glu[idx],
                                tm=tm, tm_core=tm_core)
        elif kind == 1:
            x = _gla_mixer_layer(x, g, gla_w_in[idx], gla_w_a2[idx], gla_b_a[idx], gla_norm[idx],
                                 gla_w_o[idx], tm=tm_gla)
        else:
            lambda_init = 0.8 - 0.6 * math.exp(-0.3 * layer)
            x = _diff_mixer_layer(x, g, positions, diff_w_qkv[idx], diff_lam_q1[idx],
                                  diff_lam_k1[idx], diff_lam_q2[idx], diff_lam_k2[idx],
                                  diff_subln[idx], diff_w_o[idx], lambda_init, tm=tm, tq=tq)
        x = _ffn_layer(x, norm_ffn[layer].reshape(1, D_MODEL), ffn_w_gate_up[layer],
                       ffn_w_down[layer], gfinal, final=(layer == depth - 1), tm=tm_ffn, th=th)
    return x


def kernel(x, positions, norm_mix, norm_ffn, norm_final, s5_lam_re, s5_lam_im, s5_log_dt, s5_b_re, s5_b_im, s5_c_re, s5_c_im, s5_d, s5_w_glu, gla_w_in, gla_w_a2, gla_b_a, gla_norm, gla_w_o, diff_w_qkv, diff_lam_q1, diff_lam_k1, diff_lam_q2, diff_lam_k2, diff_subln, diff_w_o, ffn_w_gate_up, ffn_w_down):
    bsz, seq, _ = x.shape
    outs = []
    for b in range(bsz):
        outs.append(_trunk(
            x[b], positions[b], norm_mix, norm_ffn, norm_final,
            s5_lam_re, s5_lam_im, s5_log_dt, s5_b_re, s5_b_im, s5_c_re, s5_c_im, s5_d, s5_w_glu,
            gla_w_in, gla_w_a2, gla_b_a, gla_norm, gla_w_o,
            diff_w_qkv, diff_lam_q1, diff_lam_k1, diff_lam_q2, diff_lam_k2, diff_subln, diff_w_o,
            ffn_w_gate_up, ffn_w_down,
            tm=512, tm_core=4096, tm_gla=512, tq=512, tm_ffn=1024, th=256))
    return jnp.stack(outs)
```

```python
import functools
import math

import jax
import jax.numpy as jnp
import numpy as np
from jax import lax
from jax.experimental import pallas as pl
from jax.experimental.pallas import tpu as pltpu

F32 = jnp.float32
BF16 = jnp.bfloat16

D_MODEL = 1024
DEPTH = 4
CHUNK = 64
N_MIXERS = 3
NORM_EPS = 1e-6
S5_GROUP = 16
S5_GROUPS = D_MODEL // S5_GROUP
S5_STATE = 64
S5_SUB = 8
S5_GB = 8
GLA_HEADS = 4
GLA_DK = D_MODEL // 2
GLA_DV = D_MODEL
GLA_DKH = GLA_DK // GLA_HEADS
GLA_DVH = GLA_DV // GLA_HEADS
GLA_GATE_RANK = 16
GLA_TEMP = 16.0
DIFF_HD = 64
DIFF_HEADS = D_MODEL // (2 * DIFF_HD)
ROT_DIMS = DIFF_HD // 4
ROPE_THETA = 500000.0
NEG_INF = -1e30
SUBLN_EPS = 1e-5
LOG2E = math.log2(math.e)
FFN_HIDDEN = -(-8 * D_MODEL // (3 * 256)) * 256

LANES = 128
VMEM_LIMIT_BYTES = 56 * 1024 * 1024


def _params(*sem):
    return pltpu.CompilerParams(dimension_semantics=sem, vmem_limit_bytes=VMEM_LIMIT_BYTES)


def _rms(x, g, eps=NORM_EPS):
    return x * lax.rsqrt(jnp.mean(x * x, axis=-1, keepdims=True) + eps) * g


def _dot(a, b):
    return jnp.dot(a, b, preferred_element_type=F32)


def _dot_nt(a, b):
    return lax.dot_general(a, b, (((1,), (1,)), ((), ())), preferred_element_type=F32)


def _dot_tn(a, b):
    return lax.dot_general(a, b, (((0,), (0,)), ((), ())), preferred_element_type=F32)


def _gelu_tanh(x):
    return x * (0.5 * (1.0 + jnp.tanh(math.sqrt(2.0 / math.pi) * (x + 0.044715 * (x * x * x)))))


def _ffn_kernel(*refs, th, final, pre):
    if pre == "s5":
        x_ref, y_ref, gm_ref, d_ref, wglu_ref, g_ref, wgu_ref, wd_ref, gf_ref, o_ref, h_scr = refs
        x = x_ref[...]
        z = _gelu_tanh(y_ref[...] + d_ref[...] * _rms(x, gm_ref[...])).astype(BF16)
        ab = _dot(z, wglu_ref[...])
        x = x + ab[:, :D_MODEL] * jax.nn.sigmoid(ab[:, D_MODEL:])
    elif pre == "proj":
        x_ref, at_ref, wo_ref, g_ref, wgu_ref, wd_ref, gf_ref, o_ref, h_scr = refs
        x = x_ref[...] + _dot_tn(at_ref[...], wo_ref[...])
    else:
        x_ref, g_ref, wgu_ref, wd_ref, gf_ref, o_ref, h_scr = refs
        x = x_ref[...]
    h_scr[...] = _rms(x, g_ref[...]).astype(BF16)
    o_ref[...] = x
    for c in range(FFN_HIDDEN // th):
        lo, hi = c * th, (c + 1) * th
        gate = _dot(h_scr[...], wgu_ref[:, lo:hi])
        up = _dot(h_scr[...], wgu_ref[:, FFN_HIDDEN + lo:FFN_HIDDEN + hi])
        act = (gate * jax.nn.sigmoid(gate) * up).astype(BF16)
        o_ref[...] += _dot(act, wd_ref[lo:hi, :])
    if final:
        o_ref[...] = _rms(o_ref[...], gf_ref[...])


def _resident(shape):
    return pl.BlockSpec(shape, lambda i: (0,) * len(shape), pipeline_mode=pl.Buffered(1))


def _ffn_layer(x, pre_args, g, w_gate_up, w_down, gf, *, final, tm, th, pre=None):
    L = x.shape[0]
    row = pl.BlockSpec((tm, D_MODEL), lambda i: (i, 0))
    vec = pl.BlockSpec((1, D_MODEL), lambda i: (0, 0))
    if pre == "s5":
        pre_specs = [row, vec, vec, _resident((D_MODEL, 2 * D_MODEL))]
    elif pre == "proj":
        pre_specs = [pl.BlockSpec((D_MODEL, tm), lambda i: (0, i)), _resident((D_MODEL, D_MODEL))]
    else:
        pre_specs = []
    return pl.pallas_call(
        functools.partial(_ffn_kernel, th=th, final=final, pre=pre),
        grid=(L // tm,),
        in_specs=[row] + pre_specs + [vec, _resident((D_MODEL, 2 * FFN_HIDDEN)),
                                      _resident((FFN_HIDDEN, D_MODEL)), vec],
        out_specs=row,
        out_shape=jax.ShapeDtypeStruct((L, D_MODEL), F32),
        scratch_shapes=[pltpu.VMEM((tm, D_MODEL), BF16)],
        compiler_params=_params("parallel"),
        name="ffn" if pre is None else "ffn_" + pre,
    )(x, *pre_args, g, w_gate_up.astype(BF16), w_down.astype(BF16), gf)


def _s5_tables(lam_re, lam_im, log_dt, b_re, b_im, c_re, c_im):
    G, P, H, T, NB = S5_GROUPS, S5_STATE, S5_GROUP, S5_SUB, S5_GB
    f32 = F32
    lr = lam_re.astype(f32)
    li = lam_im.astype(f32)
    dt = jnp.exp(log_dt.astype(f32))[:, None]
    ab_mag = jnp.exp(lr * dt)
    ab_ang = li * dt
    ab_re = ab_mag * jnp.cos(ab_ang)
    ab_im = ab_mag * jnp.sin(ab_ang)
    den = lr * lr + li * li
    f_re = ((ab_re - 1.0) * lr + ab_im * li) / den
    f_im = (ab_im * lr - (ab_re - 1.0) * li) / den
    br = b_re.astype(f32)
    bi = b_im.astype(f32)
    bb_re = f_re[..., None] * br - f_im[..., None] * bi
    bb_im = f_re[..., None] * bi + f_im[..., None] * br
    cr = c_re.astype(f32)
    ci = c_im.astype(f32)

    def powers(n):
        n = n.astype(f32)[:, None, None]
        mag = jnp.exp(lr * dt * n)
        ang = li * dt * n
        return mag * jnp.cos(ang), mag * jnp.sin(ang)

    pr, pi = powers(jnp.arange(T + 1))
    cp_re = cr[None] * pr[:, :, None, :] - ci[None] * pi[:, :, None, :]
    cp_im = cr[None] * pi[:, :, None, :] + ci[None] * pr[:, :, None, :]
    hp = lax.Precision.HIGHEST
    kern = (jnp.einsum("ngkp,gph->ngkh", cp_re[:T], bb_re, precision=hp)
            - jnp.einsum("ngkp,gph->ngkh", cp_im[:T], bb_im, precision=hp))

    wide = T * NB * H
    t_in = jnp.arange(T)[:, None]
    t_out = jnp.arange(T)[None, :]
    lag = jnp.clip(t_out - t_in, 0, T - 1)
    causal = (t_out >= t_in).astype(f32)
    xk = kern.transpose(1, 0, 3, 2).reshape(NB, NB, T, H, H)
    xk = xk.transpose(0, 2, 1, 3, 4).reshape(NB, T, NB * H, H)
    xk = xk[:, lag] * causal[None, :, :, None, None]
    w = xk.transpose(0, 1, 3, 2, 4).reshape(NB, wide, T * H)

    prq = pr[T - 1 - jnp.arange(T)]
    piq = pi[T - 1 - jnp.arange(T)]
    qv_re = prq[..., None] * bb_re[None] - piq[..., None] * bb_im[None]
    qv_im = prq[..., None] * bb_im[None] + piq[..., None] * bb_re[None]
    qv = jnp.stack([qv_re, qv_im]).reshape(2, T, NB, NB, P, H)
    q = qv.transpose(2, 1, 3, 5, 0, 4).reshape(NB, wide, 2 * P)

    pm = jnp.stack([cp_re[1:], -cp_im[1:]]).reshape(2, T, NB, NB, H, P)
    p = pm.transpose(2, 0, 3, 5, 1, 4).reshape(NB, wide, T * H)
    w, q, p = w.astype(BF16), q.astype(BF16), p.astype(BF16)

    sr, si = powers(T * jnp.arange(1, 9))
    rows = jnp.arange(8)
    kinds = []
    for sh in (1, 2, 4):
        m = (rows >= sh).astype(f32)[:, None, None]
        kinds += [m * sr[sh - 1][None], m * si[sh - 1][None]]
    kinds += [sr, si]
    mult = jnp.stack(kinds)
    mult = mult.reshape(8, 8, NB, NB * P).transpose(2, 0, 1, 3).reshape(NB, 64, NB * P)
    return w, q, p, mult


def _rms_kernel(x_ref, g_ref, o_ref):
    o_ref[...] = _rms(x_ref[...], g_ref[...])


def _rms_layer(x, g, *, tm):
    L = x.shape[0]
    return pl.pallas_call(
        _rms_kernel,
        grid=(L // tm,),
        in_specs=[pl.BlockSpec((tm, D_MODEL), lambda i: (i, 0)),
                  pl.BlockSpec((1, D_MODEL), lambda i: (0, 0))],
        out_specs=pl.BlockSpec((tm, D_MODEL), lambda i: (i, 0)),
        out_shape=jax.ShapeDtypeStruct((L, D_MODEL), F32),
        compiler_params=_params("parallel"),
        name="s5_norm",
    )(x, g)


def _s5_expanders():
    T, NB, H, P = S5_SUB, S5_GB, S5_GROUP, S5_STATE
    wide = T * NB * H
    r = np.arange(LANES)[:, None]
    c = np.arange(wide)[None, :]
    rr = np.arange(wide)[:, None]
    e_out = (r // H == c // (NB * H)) & (r % H == c % H)
    e_state = (r // P == c // (NB * P)) & (r % P == c % P)
    gl_io_r, gl_st_r = (rr // H) % NB, (rr // P) % NB
    gl_io_c, gl_st_c = (c // H) % NB, (c // P) % NB
    consts = (e_out, e_state, gl_io_r == gl_io_c, gl_io_r == gl_st_c, gl_st_r == gl_io_c)
    return [jnp.asarray(a, BF16) for a in consts]


def _s5_core_kernel(h_ref, wc_ref, qc_ref, pc_ref, m_ref, eo_ref, es_ref, mw_ref, mq_ref, mp_ref,
                    y_ref, carry_scr, b_scr, s_scr, w_scr, q_scr, p_scr, *, rows):
    half = S5_GB * S5_STATE
    strip = 2 * LANES

    @pl.when(pl.program_id(1) == 0)
    def _():
        carry_scr[...] = jnp.zeros_like(carry_scr)
        for c_ref, e_ref, k_ref, dst in ((wc_ref, eo_ref, mw_ref, w_scr), (qc_ref, es_ref, mq_ref, q_scr),
                                         (pc_ref, eo_ref, mp_ref, p_scr)):
            for c0 in range(0, S5_SUB * LANES, strip):
                cs = slice(c0, c0 + strip)
                dst[:, cs] = (_dot(c_ref[0], e_ref[:, cs]) * k_ref[:, cs]).astype(BF16)

    x = jnp.concatenate(
        [h_ref[pl.ds(t, rows, stride=S5_SUB), :] for t in range(S5_SUB)], axis=1).astype(BF16)
    b_scr[...] = _dot(x, q_scr[...])

    first_row = lax.broadcasted_iota(jnp.int32, (8, half), 0) == 0
    c_re = carry_scr[:, :half]
    c_im = carry_scr[:, half:]
    for tile in range(rows // 8):
        r0 = tile * 8
        x_re = b_scr[r0:r0 + 8, :half]
        x_im = b_scr[r0:r0 + 8, half:]
        for lvl, sh in enumerate((1, 2, 4)):
            a_re = m_ref[0, 16 * lvl:16 * lvl + 8, :]
            a_im = m_ref[0, 16 * lvl + 8:16 * lvl + 16, :]
            r_re = pltpu.roll(x_re, sh, 0)
            r_im = pltpu.roll(x_im, sh, 0)
            x_re, x_im = (x_re + a_re * r_re - a_im * r_im,
                          x_im + a_re * r_im + a_im * r_re)
        p_re = m_ref[0, 48:56, :]
        p_im = m_ref[0, 56:64, :]
        s_re = x_re + p_re * c_re - p_im * c_im
        s_im = x_im + p_re * c_im + p_im * c_re
        s_scr[r0:r0 + 8, :half] = jnp.where(first_row, c_re, pltpu.roll(s_re, 1, 0))
        s_scr[r0:r0 + 8, half:] = jnp.where(first_row, c_im, pltpu.roll(s_im, 1, 0))
        c_re = jnp.broadcast_to(s_re[7:8, :], (8, half))
        c_im = jnp.broadcast_to(s_im[7:8, :], (8, half))
    carry_scr[:, :half] = c_re
    carry_scr[:, half:] = c_im

    y = _dot(x, w_scr[...]) + _dot(s_scr[...].astype(BF16), p_scr[...])
    for t in range(S5_SUB):
        y_ref[pl.ds(t, rows, stride=S5_SUB), :] = y[:, t * LANES:(t + 1) * LANES]


def _s5_core(h, w, q, p, mult, *, tm):
    L = h.shape[0]
    rows = tm // S5_SUB
    wide = S5_SUB * LANES
    compact = pl.BlockSpec((1, wide, LANES), lambda g, i: (g, 0, 0))
    expand = pl.BlockSpec((LANES, wide), lambda g, i: (0, 0), pipeline_mode=pl.Buffered(1))
    mask = pl.BlockSpec((wide, wide), lambda g, i: (0, 0), pipeline_mode=pl.Buffered(1))
    table = pltpu.VMEM((wide, wide), BF16)
    return pl.pallas_call(
        functools.partial(_s5_core_kernel, rows=rows),
        grid=(S5_GB, L // tm),
        in_specs=[
            pl.BlockSpec((tm, LANES), lambda g, i: (i, g)),
            compact, compact, compact,
            pl.BlockSpec((1, 64, S5_GB * S5_STATE), lambda g, i: (g, 0, 0)),
            expand, expand, mask, mask, mask,
        ],
        out_specs=pl.BlockSpec((tm, LANES), lambda g, i: (i, g)),
        out_shape=jax.ShapeDtypeStruct((L, D_MODEL), F32),
        scratch_shapes=[pltpu.VMEM((8, wide), F32), pltpu.VMEM((rows, wide), F32),
                        pltpu.VMEM((rows, wide), F32), table, table, table],
        compiler_params=_params("parallel", "arbitrary"),
        name="s5_core",
    )(h, w, q, p, mult, *_s5_expanders())


def _s5_mixer_layer(x, g, lam_re, lam_im, log_dt, b_re, b_im, c_re, c_im, d_skip, w_glu, *, tm, tm_core):
    w, q, p, mult = _s5_tables(lam_re, lam_im, log_dt, b_re, b_im, c_re, c_im)
    h = _rms_layer(x, g, tm=tm)
    y = _s5_core(h, w, q, p, mult, tm=tm_core)
    return y, g, d_skip.reshape(1, D_MODEL), w_glu.astype(BF16)


def _gla_kernel(x_ref, g_ref, wq_ref, wk_ref, wv_ref, wg_ref, wa1_ref, wa2_ref, ba_ref, ng_ref,
                wo_ref, tri_ref, o_ref, st_scr, o_scr, upd_scr, stb_scr, *, tm):
    @pl.when(pl.program_id(0) == 0)
    def _():
        st_scr[...] = jnp.zeros_like(st_scr)

    x = x_ref[...]
    h = _rms(x, g_ref[...]).astype(BF16)
    q = (_dot(h, wq_ref[...]) * (GLA_DKH ** -0.5)).astype(BF16)
    k = _dot(h, wk_ref[...])
    v = _dot(h, wv_ref[...]).astype(BF16)
    gate = _dot(h, wg_ref[...])
    a_lo = _dot(h, wa1_ref[...]).astype(BF16)
    log_a = jax.nn.log_sigmoid(_dot(a_lo, wa2_ref[...]) + ba_ref[...]) / GLA_TEMP

    la_hi = log_a.astype(BF16)
    la_lo = (log_a - la_hi.astype(F32)).astype(BF16)
    cum = _dot(tri_ref[...], la_hi) + _dot(tri_ref[...], la_lo)
    n_chunks = tm // CHUNK
    tot_rows = [cum[(c + 1) * CHUNK - 1:(c + 1) * CHUNK, :] for c in range(n_chunks)]
    tot = jnp.concatenate([jnp.broadcast_to(t, (CHUNK, GLA_DK)) for t in tot_rows], axis=0)
    k_dec = (k * jnp.exp(tot - cum)).astype(BF16)

    heads = [(slice(hd * GLA_DKH, (hd + 1) * GLA_DKH), slice(hd * GLA_DVH, (hd + 1) * GLA_DVH))
             for hd in range(GLA_HEADS)]
    for c in range(n_chunks):
        rs = slice(c * CHUNK, (c + 1) * CHUNK)
        for hd, (ks, vs) in enumerate(heads):
            upd_scr[c, hd] = _dot_tn(v[rs, vs], k_dec[rs, ks])
    for c in range(n_chunks):
        dec_row = jnp.exp(tot_rows[c])
        for hd, (ks, vs) in enumerate(heads):
            st = st_scr[hd] * dec_row[:, ks] + upd_scr[c, hd]
            st_scr[hd] = st
            stb_scr[c, hd] = st.astype(BF16)
    for c in range(n_chunks):
        rs = slice(c * CHUNK, (c + 1) * CHUNK)
        for hd, (ks, vs) in enumerate(heads):
            o_scr[rs, vs] = _dot_nt(q[rs, ks], stb_scr[c, hd])

    outs = []
    for hd in range(GLA_HEADS):
        vs = slice(hd * GLA_DVH, (hd + 1) * GLA_DVH)
        o = o_scr[:, vs]
        outs.append(o * lax.rsqrt(jnp.mean(o * o, axis=-1, keepdims=True) + NORM_EPS) * ng_ref[:, vs])
    o = jnp.concatenate(outs, axis=1) * (gate * jax.nn.sigmoid(gate))
    o_ref[...] = x + _dot(o.astype(BF16), wo_ref[...])


def _gla_mixer_layer(x, g, w_in, w_a2, b_a, norm_g, w_o, *, tm):
    L = x.shape[0]
    w_in = w_in.astype(BF16)
    wq = w_in[:, :GLA_DK]
    wk = w_in[:, GLA_DK:2 * GLA_DK]
    wv = w_in[:, 2 * GLA_DK:2 * GLA_DK + GLA_DV]
    wg = w_in[:, 2 * GLA_DK + GLA_DV:2 * GLA_DK + 2 * GLA_DV]
    wa1 = jnp.pad(w_in[:, 2 * GLA_DK + 2 * GLA_DV:], ((0, 0), (0, LANES - GLA_GATE_RANK)))
    wa2 = jnp.pad(w_a2.astype(BF16), ((0, LANES - GLA_GATE_RANK), (0, 0)))
    r = jnp.arange(tm)
    tri = (((r[:, None] // CHUNK) == (r[None, :] // CHUNK)) & (r[None, :] <= r[:, None])).astype(BF16)
    full = _resident
    n_chunks = tm // CHUNK

    row = pl.BlockSpec((tm, D_MODEL), lambda i: (i, 0))
    return pl.pallas_call(
        functools.partial(_gla_kernel, tm=tm),
        grid=(L // tm,),
        in_specs=[row, full((1, D_MODEL)), full((D_MODEL, GLA_DK)), full((D_MODEL, GLA_DK)),
                  full((D_MODEL, GLA_DV)), full((D_MODEL, GLA_DV)), full((D_MODEL, LANES)),
                  full((LANES, GLA_DK)), full((1, GLA_DK)), full((1, GLA_DV)),
                  full((GLA_DV, D_MODEL)), full((tm, tm))],
        out_specs=row,
        out_shape=jax.ShapeDtypeStruct((L, D_MODEL), F32),
        scratch_shapes=[pltpu.VMEM((GLA_HEADS, GLA_DVH, GLA_DKH), F32),
                        pltpu.VMEM((tm, GLA_DV), F32),
                        pltpu.VMEM((n_chunks, GLA_HEADS, GLA_DVH, GLA_DKH), F32),
                        pltpu.VMEM((n_chunks, GLA_HEADS, GLA_DVH, GLA_DKH), BF16)],
        compiler_params=_params("arbitrary"),
        name="gla",
    )(x, g, wq, wk, wv, wg, wa1, wa2, b_a.reshape(1, GLA_DK), norm_g.reshape(1, GLA_DV),
      w_o.astype(BF16), tri)


def _rope_tables(positions):
    half = ROT_DIMS // 2
    inv_freq = ROPE_THETA ** (-jnp.arange(half, dtype=F32) / half)
    ang = positions.astype(F32)[:, None] * inv_freq
    cos = jnp.cos(ang)
    sin = jnp.sin(ang)
    m = jnp.arange(LANES) % DIFF_HD
    cos_l = cos[:, m % half]
    sin_l = sin[:, m % half]
    c = jnp.where(m < ROT_DIMS, cos_l, 1.0)
    s_lo = jnp.where(m < half, -sin_l, 0.0)
    s_hi = jnp.where((m >= half) & (m < ROT_DIMS), sin_l, 0.0)
    return c, s_lo, s_hi


def _qkv_kernel(x_ref, g_ref, w_ref, c_ref, slo_ref, shi_ref, qt_ref, k_ref, vt_ref):
    h = _rms(x_ref[...], g_ref[...]).astype(BF16)
    qkv = _dot(h, w_ref[...])
    c = c_ref[...]
    s_lo = slo_ref[...]
    s_hi = shi_ref[...]
    half = ROT_DIMS // 2

    def rope(t):
        return t * c + pltpu.roll(t, LANES - half, 1) * s_lo + pltpu.roll(t, half, 1) * s_hi

    for j in range(D_MODEL // LANES):
        ls = slice(j * LANES, (j + 1) * LANES)
        q = rope(qkv[:, ls]) * (DIFF_HD ** -0.5 * LOG2E)
        qt_ref[ls, :] = q.T.astype(BF16)
        k_ref[j, 0] = rope(qkv[:, D_MODEL + j * LANES:D_MODEL + (j + 1) * LANES]).astype(BF16)
        vt_ref[j, 0] = qkv[:, 2 * D_MODEL + j * LANES:2 * D_MODEL + (j + 1) * LANES].T.astype(BF16)


def _qkv_layer(x, g, w_qkv, c, s_lo, s_hi, *, tm):
    L = x.shape[0]
    n = L // tm
    row = pl.BlockSpec((tm, D_MODEL), lambda i: (i, 0))
    tab = pl.BlockSpec((tm, LANES), lambda i: (i, 0))
    return pl.pallas_call(
        _qkv_kernel,
        grid=(n,),
        in_specs=[row, pl.BlockSpec((1, D_MODEL), lambda i: (0, 0)),
                  _resident((D_MODEL, 3 * D_MODEL)), tab, tab, tab],
        out_specs=[pl.BlockSpec((D_MODEL, tm), lambda i: (0, i)),
                   pl.BlockSpec((DIFF_HEADS, 1, tm, LANES), lambda i: (0, i, 0, 0)),
                   pl.BlockSpec((DIFF_HEADS, 1, LANES, tm), lambda i: (0, i, 0, 0))],
        out_shape=[jax.ShapeDtypeStruct((D_MODEL, L), BF16),
                   jax.ShapeDtypeStruct((DIFF_HEADS, n, tm, LANES), BF16),
                   jax.ShapeDtypeStruct((DIFF_HEADS, n, LANES, tm), BF16)],
        compiler_params=_params("parallel"),
        name="diff_qkv",
    )(x, g, w_qkv, c, s_lo, s_hi)


def _attn_kernel(qt_ref, k_ref, vt_ref, lamv_ref, sg_ref, o_ref, m_scr, l_scr, acc_scr,
                 sa_scr, sb_scr, *, tq, lambda_init):
    i = pl.program_id(1)
    qt = qt_ref[...]
    row = lax.broadcasted_iota(jnp.int32, qt.shape, 0)
    zero = jnp.zeros_like(qt)
    q_half = (jnp.where(row < DIFF_HD, qt, zero), jnp.where(row >= DIFF_HD, qt, zero))

    m_scr[...] = jnp.full_like(m_scr, NEG_INF)
    l_scr[...] = jnp.zeros_like(l_scr)
    acc_scr[...] = jnp.zeros_like(acc_scr)

    def scores(j, s_ref):
        k = k_ref[0, j]
        for hf in range(2):
            s_ref[hf] = _dot(k, q_half[hf])

    def update(j, s_ref, masked):
        vt = vt_ref[0, j]
        if masked:
            kc = lax.broadcasted_iota(jnp.int32, (tq, tq), 0) // CHUNK
            qc = lax.broadcasted_iota(jnp.int32, (tq, tq), 1) // CHUNK
            visible = kc <= qc
        for hf in range(2):
            s = s_ref[hf]
            if masked:
                s = jnp.where(visible, s, NEG_INF)
            m_prev = m_scr[hf]
            m_new = jnp.maximum(m_prev, jnp.max(s, axis=0, keepdims=True))
            alpha = jnp.exp2(m_prev - m_new)
            p = jnp.exp2(s - m_new)
            l_scr[hf] = alpha * l_scr[hf] + jnp.sum(p, axis=0, keepdims=True)
            acc_scr[hf] = alpha * acc_scr[hf] + _dot(vt, p.astype(BF16))
            m_scr[hf] = m_new

    scores(0, sa_scr)

    def pair(j):
        scores(j + 1, sb_scr)
        update(j, sa_scr, False)
        scores(j + 2, sa_scr)
        update(j + 1, sb_scr, False)

    def quad(t, carry):
        pair(4 * t)
        pair(4 * t + 2)
        return carry

    lax.fori_loop(0, i // 4, quad, 0)

    @pl.when((i // 2) % 2 == 1)
    def _():
        pair((i // 4) * 4)

    @pl.when(i % 2 == 0)
    def _():
        update(i, sa_scr, True)

    @pl.when(i % 2 == 1)
    def _():
        scores(i, sb_scr)
        update(i - 1, sa_scr, False)
        update(i, sb_scr, True)

    lv = lamv_ref[...]
    lam = (jnp.exp(jnp.sum(lv[0:1] * lv[1:2], axis=-1, keepdims=True))
           - jnp.exp(jnp.sum(lv[2:3] * lv[3:4], axis=-1, keepdims=True)) + lambda_init)
    o = acc_scr[0] / l_scr[0] - lam * (acc_scr[1] / l_scr[1])
    o = o * lax.rsqrt(jnp.mean(o * o, axis=0, keepdims=True) + SUBLN_EPS) * sg_ref[...]
    o_ref[...] = (o * (1.0 - lambda_init)).astype(BF16)


def _attn_layer(qt, k, vt, lamv, subln_g, *, tq, lambda_init):
    n = k.shape[1]
    L = n * tq
    stat = pltpu.VMEM((2, 1, tq), F32)
    return pl.pallas_call(
        functools.partial(_attn_kernel, tq=tq, lambda_init=lambda_init),
        grid=(DIFF_HEADS, n),
        in_specs=[
            pl.BlockSpec((LANES, tq), lambda p, i: (p, i)),
            pl.BlockSpec((1, n, tq, LANES), lambda p, i: (p, 0, 0, 0)),
            pl.BlockSpec((1, n, LANES, tq), lambda p, i: (p, 0, 0, 0)),
            pl.BlockSpec((4, DIFF_HD), lambda p, i: (0, 0)),
            pl.BlockSpec((LANES, 1), lambda p, i: (0, 0)),
        ],
        out_specs=pl.BlockSpec((LANES, tq), lambda p, i: (p, i)),
        out_shape=jax.ShapeDtypeStruct((D_MODEL, L), BF16),
        scratch_shapes=[stat, stat, pltpu.VMEM((2, LANES, tq), F32),
                        pltpu.VMEM((2, tq, tq), F32), pltpu.VMEM((2, tq, tq), F32)],
        compiler_params=_params("parallel", "arbitrary"),
        name="diff_attn",
    )(qt, k, vt, lamv, subln_g)


def _diff_mixer_layer(x, g, positions, w_qkv, lam_q1, lam_k1, lam_q2, lam_k2, subln_g, w_o,
                      lambda_init, *, tq):
    c, s_lo, s_hi = _rope_tables(positions)
    qt, k4, vt4 = _qkv_layer(x, g, w_qkv.astype(BF16), c, s_lo, s_hi, tm=tq)
    lamv = jnp.stack([lam_q1, lam_k1, lam_q2, lam_k2]).astype(F32)
    at = _attn_layer(qt, k4, vt4, lamv, subln_g.reshape(LANES, 1).astype(F32), tq=tq,
                     lambda_init=lambda_init)
    return at, w_o.astype(BF16)


def _trunk(x, positions, norm_mix, norm_ffn, norm_final,
           s5_lam_re, s5_lam_im, s5_log_dt, s5_b_re, s5_b_im, s5_c_re, s5_c_im, s5_d, s5_w_glu,
           gla_w_in, gla_w_a2, gla_b_a, gla_norm, gla_w_o,
           diff_w_qkv, diff_lam_q1, diff_lam_k1, diff_lam_q2, diff_lam_k2, diff_subln, diff_w_o,
           ffn_w_gate_up, ffn_w_down, *, tm, tm_core, tm_gla, tq, tm_ffn, th):
    depth = norm_mix.shape[0]
    gfinal = norm_final.reshape(1, D_MODEL)
    for layer in range(depth):
        kind = layer % N_MIXERS
        idx = layer // N_MIXERS
        g = norm_mix[layer].reshape(1, D_MODEL)
        if kind == 0:
            pre, tm_f = "s5", tm
            pre_args = _s5_mixer_layer(x, g, s5_lam_re[idx], s5_lam_im[idx], s5_log_dt[idx],
                                       s5_b_re[idx], s5_b_im[idx], s5_c_re[idx], s5_c_im[idx],
                                       s5_d[idx], s5_w_glu[idx], tm=tm, tm_core=tm_core)
        elif kind == 1:
            pre, tm_f, pre_args = None, tm_ffn, ()
            x = _gla_mixer_layer(x, g, gla_w_in[idx], gla_w_a2[idx], gla_b_a[idx], gla_norm[idx],
                                 gla_w_o[idx], tm=tm_gla)
        else:
            pre, tm_f = "proj", tm
            lambda_init = 0.8 - 0.6 * math.exp(-0.3 * layer)
            pre_args = _diff_mixer_layer(x, g, positions, diff_w_qkv[idx], diff_lam_q1[idx],
                                         diff_lam_k1[idx], diff_lam_q2[idx], diff_lam_k2[idx],
                                         diff_subln[idx], diff_w_o[idx], lambda_init, tq=tq)
        x = _ffn_layer(x, pre_args, norm_ffn[layer].reshape(1, D_MODEL), ffn_w_gate_up[layer],
                       ffn_w_down[layer], gfinal, final=(layer == depth - 1), tm=tm_f, th=th, pre=pre)
    return x


def kernel(x, positions, norm_mix, norm_ffn, norm_final, s5_lam_re, s5_lam_im, s5_log_dt, s5_b_re, s5_b_im, s5_c_re, s5_c_im, s5_d, s5_w_glu, gla_w_in, gla_w_a2, gla_b_a, gla_norm, gla_w_o, diff_w_qkv, diff_lam_q1, diff_lam_k1, diff_lam_q2, diff_lam_k2, diff_subln, diff_w_o, ffn_w_gate_up, ffn_w_down):
    bsz, seq, _ = x.shape
    outs = []
    for b in range(bsz):
        outs.append(_trunk(
            x[b], positions[b], norm_mix, norm_ffn, norm_final,
            s5_lam_re, s5_lam_im, s5_log_dt, s5_b_re, s5_b_im, s5_c_re, s5_c_im, s5_d, s5_w_glu,
            gla_w_in, gla_w_a2, gla_b_a, gla_norm, gla_w_o,
            diff_w_qkv, diff_lam_q1, diff_lam_k1, diff_lam_q2, diff_lam_k2, diff_subln, diff_w_o,
            ffn_w_gate_up, ffn_w_down,
            tm=512, tm_core=4096, tm_gla=512, tq=512, tm_ffn=1024, th=256))
    return jnp.stack(outs)
```

```python
import functools
import math

import jax
import jax.numpy as jnp
import numpy as np
from jax import lax
from jax.experimental import pallas as pl
from jax.experimental.pallas import tpu as pltpu

F32 = jnp.float32
BF16 = jnp.bfloat16

D_MODEL = 1024
DEPTH = 4
CHUNK = 64
N_MIXERS = 3
NORM_EPS = 1e-6
S5_GROUP = 16
S5_GROUPS = D_MODEL // S5_GROUP
S5_STATE = 64
S5_SUB = 8
S5_GB = 8
GLA_HEADS = 4
GLA_DK = D_MODEL // 2
GLA_DV = D_MODEL
GLA_DKH = GLA_DK // GLA_HEADS
GLA_DVH = GLA_DV // GLA_HEADS
GLA_GATE_RANK = 16
GLA_TEMP = 16.0
DIFF_HD = 64
DIFF_HEADS = D_MODEL // (2 * DIFF_HD)
ROT_DIMS = DIFF_HD // 4
ROPE_THETA = 500000.0
NEG_INF = -1e30
SUBLN_EPS = 1e-5
LOG2E = math.log2(math.e)
FFN_HIDDEN = -(-8 * D_MODEL // (3 * 256)) * 256

LANES = 128
VMEM_LIMIT_BYTES = 56 * 1024 * 1024


def _params(*sem):
    return pltpu.CompilerParams(dimension_semantics=sem, vmem_limit_bytes=VMEM_LIMIT_BYTES)


def _rms(x, g, eps=NORM_EPS):
    return x * lax.rsqrt(jnp.mean(x * x, axis=-1, keepdims=True) + eps) * g


def _dot(a, b):
    return jnp.dot(a, b, preferred_element_type=F32)


def _dot_nt(a, b):
    return lax.dot_general(a, b, (((1,), (1,)), ((), ())), preferred_element_type=F32)


def _dot_tn(a, b):
    return lax.dot_general(a, b, (((0,), (0,)), ((), ())), preferred_element_type=F32)


def _gelu_tanh(x):
    return x * (0.5 * (1.0 + jnp.tanh(math.sqrt(2.0 / math.pi) * (x + 0.044715 * (x * x * x)))))


def _ffn_kernel(*refs, th, final, pre):
    if pre == "s5":
        x_ref, z_ref, wglu_ref, g_ref, wgu_ref, wd_ref, gf_ref, o_ref, h_scr = refs
        ab = _dot(z_ref[...].astype(BF16), wglu_ref[...])
        x = x_ref[...] + ab[:, :D_MODEL] * jax.nn.sigmoid(ab[:, D_MODEL:])
    elif pre == "proj":
        x_ref, at_ref, wo_ref, g_ref, wgu_ref, wd_ref, gf_ref, o_ref, h_scr = refs
        x = x_ref[...] + _dot_tn(at_ref[...], wo_ref[...])
    else:
        x_ref, g_ref, wgu_ref, wd_ref, gf_ref, o_ref, h_scr = refs
        x = x_ref[...]
    h_scr[...] = _rms(x, g_ref[...]).astype(BF16)
    o_ref[...] = x
    for c in range(FFN_HIDDEN // th):
        lo, hi = c * th, (c + 1) * th
        gate = _dot(h_scr[...], wgu_ref[:, lo:hi])
        up = _dot(h_scr[...], wgu_ref[:, FFN_HIDDEN + lo:FFN_HIDDEN + hi])
        act = (gate * jax.nn.sigmoid(gate) * up).astype(BF16)
        o_ref[...] += _dot(act, wd_ref[lo:hi, :])
    if final:
        o_ref[...] = _rms(o_ref[...], gf_ref[...])


def _resident(shape):
    return pl.BlockSpec(shape, lambda i: (0,) * len(shape), pipeline_mode=pl.Buffered(1))


def _ffn_layer(x, pre_args, g, w_gate_up, w_down, gf, *, final, tm, th, pre=None):
    L = x.shape[0]
    row = pl.BlockSpec((tm, D_MODEL), lambda i: (i, 0))
    vec = pl.BlockSpec((1, D_MODEL), lambda i: (0, 0))
    if pre == "s5":
        pre_specs = [row, _resident((D_MODEL, 2 * D_MODEL))]
    elif pre == "proj":
        pre_specs = [pl.BlockSpec((D_MODEL, tm), lambda i: (0, i)), _resident((D_MODEL, D_MODEL))]
    else:
        pre_specs = []
    return pl.pallas_call(
        functools.partial(_ffn_kernel, th=th, final=final, pre=pre),
        grid=(L // tm,),
        in_specs=[row] + pre_specs + [vec, _resident((D_MODEL, 2 * FFN_HIDDEN)),
                                      _resident((FFN_HIDDEN, D_MODEL)), vec],
        out_specs=row,
        out_shape=jax.ShapeDtypeStruct((L, D_MODEL), F32),
        scratch_shapes=[pltpu.VMEM((tm, D_MODEL), BF16)],
        compiler_params=_params("parallel"),
        name="ffn" if pre is None else "ffn_" + pre,
    )(x, *pre_args, g, w_gate_up.astype(BF16), w_down.astype(BF16), gf)


def _s5_tables(lam_re, lam_im, log_dt, b_re, b_im, c_re, c_im):
    G, P, H, T, NB = S5_GROUPS, S5_STATE, S5_GROUP, S5_SUB, S5_GB
    f32 = F32
    lr = lam_re.astype(f32)
    li = lam_im.astype(f32)
    dt = jnp.exp(log_dt.astype(f32))[:, None]
    ab_mag = jnp.exp(lr * dt)
    ab_ang = li * dt
    ab_re = ab_mag * jnp.cos(ab_ang)
    ab_im = ab_mag * jnp.sin(ab_ang)
    den = lr * lr + li * li
    f_re = ((ab_re - 1.0) * lr + ab_im * li) / den
    f_im = (ab_im * lr - (ab_re - 1.0) * li) / den
    br = b_re.astype(f32)
    bi = b_im.astype(f32)
    bb_re = f_re[..., None] * br - f_im[..., None] * bi
    bb_im = f_re[..., None] * bi + f_im[..., None] * br
    cr = c_re.astype(f32)
    ci = c_im.astype(f32)

    def powers(n):
        n = n.astype(f32)[:, None, None]
        mag = jnp.exp(lr * dt * n)
        ang = li * dt * n
        return mag * jnp.cos(ang), mag * jnp.sin(ang)

    pr, pi = powers(jnp.arange(T + 1))
    cp_re = cr[None] * pr[:, :, None, :] - ci[None] * pi[:, :, None, :]
    cp_im = cr[None] * pi[:, :, None, :] + ci[None] * pr[:, :, None, :]
    hp = lax.Precision.HIGHEST
    kern = (jnp.einsum("ngkp,gph->ngkh", cp_re[:T], bb_re, precision=hp)
            - jnp.einsum("ngkp,gph->ngkh", cp_im[:T], bb_im, precision=hp))

    wide = T * NB * H
    t_in = jnp.arange(T)[:, None]
    t_out = jnp.arange(T)[None, :]
    lag = jnp.clip(t_out - t_in, 0, T - 1)
    causal = (t_out >= t_in).astype(f32)
    xk = kern.transpose(1, 0, 3, 2).reshape(NB, NB, T, H, H)
    xk = xk.transpose(0, 2, 1, 3, 4).reshape(NB, T, NB * H, H)
    xk = xk[:, lag] * causal[None, :, :, None, None]
    w = xk.transpose(0, 1, 3, 2, 4).reshape(NB, wide, T * H)

    prq = pr[T - 1 - jnp.arange(T)]
    piq = pi[T - 1 - jnp.arange(T)]
    qv_re = prq[..., None] * bb_re[None] - piq[..., None] * bb_im[None]
    qv_im = prq[..., None] * bb_im[None] + piq[..., None] * bb_re[None]
    qv = jnp.stack([qv_re, qv_im]).reshape(2, T, NB, NB, P, H)
    q = qv.transpose(2, 1, 3, 5, 0, 4).reshape(NB, wide, 2 * P)

    pm = jnp.stack([cp_re[1:], -cp_im[1:]]).reshape(2, T, NB, NB, H, P)
    p = pm.transpose(2, 0, 3, 5, 1, 4).reshape(NB, wide, T * H)
    w, q, p = w.astype(BF16), q.astype(BF16), p.astype(BF16)

    sr, si = powers(T * jnp.arange(1, 9))
    rows = jnp.arange(8)
    kinds = []
    for sh in (1, 2, 4):
        m = (rows >= sh).astype(f32)[:, None, None]
        kinds += [m * sr[sh - 1][None], m * si[sh - 1][None]]
    kinds += [sr, si]
    mult = jnp.stack(kinds)
    mult = mult.reshape(8, 8, NB, NB * P).transpose(2, 0, 1, 3).reshape(NB, 64, NB * P)
    return w, q, p, mult


def _rms_kernel(x_ref, g_ref, o_ref):
    o_ref[...] = _rms(x_ref[...], g_ref[...])


def _rms_layer(x, g, *, tm):
    L = x.shape[0]
    return pl.pallas_call(
        _rms_kernel,
        grid=(L // tm,),
        in_specs=[pl.BlockSpec((tm, D_MODEL), lambda i: (i, 0)),
                  pl.BlockSpec((1, D_MODEL), lambda i: (0, 0))],
        out_specs=pl.BlockSpec((tm, D_MODEL), lambda i: (i, 0)),
        out_shape=jax.ShapeDtypeStruct((L, D_MODEL), F32),
        compiler_params=_params("parallel"),
        name="s5_norm",
    )(x, g)


def _s5_expanders():
    T, NB, H, P = S5_SUB, S5_GB, S5_GROUP, S5_STATE
    wide = T * NB * H
    r = np.arange(LANES)[:, None]
    c = np.arange(wide)[None, :]
    rr = np.arange(wide)[:, None]
    e_out = (r // H == c // (NB * H)) & (r % H == c % H)
    e_state = (r // P == c // (NB * P)) & (r % P == c % P)
    gl_io_r, gl_st_r = (rr // H) % NB, (rr // P) % NB
    gl_io_c, gl_st_c = (c // H) % NB, (c // P) % NB
    consts = (e_out, e_state, gl_io_r == gl_io_c, gl_io_r == gl_st_c, gl_st_r == gl_io_c)
    return [jnp.asarray(a, BF16) for a in consts]


def _s5_core_kernel(h_ref, wc_ref, qc_ref, pc_ref, m_ref, d_ref, eo_ref, es_ref, mw_ref, mq_ref, mp_ref,
                    z_ref, carry_scr, b_scr, s_scr, w_scr, q_scr, p_scr, *, rows):
    half = S5_GB * S5_STATE
    strip = 2 * LANES

    @pl.when(pl.program_id(1) == 0)
    def _():
        carry_scr[...] = jnp.zeros_like(carry_scr)
        for c_ref, e_ref, k_ref, dst in ((wc_ref, eo_ref, mw_ref, w_scr), (qc_ref, es_ref, mq_ref, q_scr),
                                         (pc_ref, eo_ref, mp_ref, p_scr)):
            for c0 in range(0, S5_SUB * LANES, strip):
                cs = slice(c0, c0 + strip)
                dst[:, cs] = (_dot(c_ref[0], e_ref[:, cs]) * k_ref[:, cs]).astype(BF16)

    h_t = [h_ref[pl.ds(t, rows, stride=S5_SUB), :] for t in range(S5_SUB)]
    x = jnp.concatenate(h_t, axis=1).astype(BF16)
    b_scr[...] = _dot(x, q_scr[...])

    first_row = lax.broadcasted_iota(jnp.int32, (8, half), 0) == 0
    c_re = carry_scr[:, :half]
    c_im = carry_scr[:, half:]
    for tile in range(rows // 8):
        r0 = tile * 8
        x_re = b_scr[r0:r0 + 8, :half]
        x_im = b_scr[r0:r0 + 8, half:]
        for lvl, sh in enumerate((1, 2, 4)):
            a_re = m_ref[0, 16 * lvl:16 * lvl + 8, :]
            a_im = m_ref[0, 16 * lvl + 8:16 * lvl + 16, :]
            r_re = pltpu.roll(x_re, sh, 0)
            r_im = pltpu.roll(x_im, sh, 0)
            x_re, x_im = (x_re + a_re * r_re - a_im * r_im,
                          x_im + a_re * r_im + a_im * r_re)
        p_re = m_ref[0, 48:56, :]
        p_im = m_ref[0, 56:64, :]
        s_re = x_re + p_re * c_re - p_im * c_im
        s_im = x_im + p_re * c_im + p_im * c_re
        s_scr[r0:r0 + 8, :half] = jnp.where(first_row, c_re, pltpu.roll(s_re, 1, 0))
        s_scr[r0:r0 + 8, half:] = jnp.where(first_row, c_im, pltpu.roll(s_im, 1, 0))
        c_re = jnp.broadcast_to(s_re[7:8, :], (8, half))
        c_im = jnp.broadcast_to(s_im[7:8, :], (8, half))
    carry_scr[:, :half] = c_re
    carry_scr[:, half:] = c_im

    s = s_scr[...].astype(BF16)
    for c0 in range(0, S5_SUB * LANES, strip):
        y = _dot(x[:, :c0 + strip], w_scr[:c0 + strip, c0:c0 + strip]) + _dot(s, p_scr[:, c0:c0 + strip])
        for t in range(c0 // LANES, (c0 + strip) // LANES):
            y_t = y[:, t * LANES - c0:(t + 1) * LANES - c0] + d_ref[...] * h_t[t]
            z_ref[pl.ds(t, rows, stride=S5_SUB), :] = _gelu_tanh(y_t)


def _s5_core(h, w, q, p, mult, d_skip, *, tm):
    L = h.shape[0]
    rows = tm // S5_SUB
    wide = S5_SUB * LANES
    compact = pl.BlockSpec((1, wide, LANES), lambda g, i: (g, 0, 0))
    expand = pl.BlockSpec((LANES, wide), lambda g, i: (0, 0), pipeline_mode=pl.Buffered(1))
    mask = pl.BlockSpec((wide, wide), lambda g, i: (0, 0), pipeline_mode=pl.Buffered(1))
    table = pltpu.VMEM((wide, wide), BF16)
    return pl.pallas_call(
        functools.partial(_s5_core_kernel, rows=rows),
        grid=(S5_GB, L // tm),
        in_specs=[
            pl.BlockSpec((tm, LANES), lambda g, i: (i, g)),
            compact, compact, compact,
            pl.BlockSpec((1, 64, S5_GB * S5_STATE), lambda g, i: (g, 0, 0)),
            pl.BlockSpec((1, LANES), lambda g, i: (0, g)),
            expand, expand, mask, mask, mask,
        ],
        out_specs=pl.BlockSpec((tm, LANES), lambda g, i: (i, g)),
        out_shape=jax.ShapeDtypeStruct((L, D_MODEL), F32),
        scratch_shapes=[pltpu.VMEM((8, wide), F32), pltpu.VMEM((rows, wide), F32),
                        pltpu.VMEM((rows, wide), F32), table, table, table],
        compiler_params=_params("parallel", "arbitrary"),
        name="s5_core",
    )(h, w, q, p, mult, d_skip, *_s5_expanders())


def _s5_mixer_layer(x, g, lam_re, lam_im, log_dt, b_re, b_im, c_re, c_im, d_skip, w_glu, *, tm, tm_core):
    w, q, p, mult = _s5_tables(lam_re, lam_im, log_dt, b_re, b_im, c_re, c_im)
    h = _rms_layer(x, g, tm=tm)
    z = _s5_core(h, w, q, p, mult, d_skip.reshape(1, D_MODEL), tm=tm_core)
    return z, w_glu.astype(BF16)


def _gla_kernel(x_ref, g_ref, wq_ref, wk_ref, wv_ref, wg_ref, wa1_ref, wa2_ref, ba_ref, ng_ref,
                wo_ref, tri_ref, o_ref, st_scr, o_scr, upd_scr, stb_scr, *, tm):
    @pl.when(pl.program_id(0) == 0)
    def _():
        st_scr[...] = jnp.zeros_like(st_scr)

    x = x_ref[...]
    h = _rms(x, g_ref[...]).astype(BF16)
    q = (_dot(h, wq_ref[...]) * (GLA_DKH ** -0.5)).astype(BF16)
    k = _dot(h, wk_ref[...])
    v = _dot(h, wv_ref[...]).astype(BF16)
    gate = _dot(h, wg_ref[...])
    a_lo = _dot(h, wa1_ref[...]).astype(BF16)
    log_a = jax.nn.log_sigmoid(_dot(a_lo, wa2_ref[...]) + ba_ref[...]) / GLA_TEMP

    la_hi = log_a.astype(BF16)
    la_lo = (log_a - la_hi.astype(F32)).astype(BF16)
    cum = _dot(tri_ref[...], la_hi) + _dot(tri_ref[...], la_lo)
    n_chunks = tm // CHUNK
    tot_rows = [cum[(c + 1) * CHUNK - 1:(c + 1) * CHUNK, :] for c in range(n_chunks)]
    tot = jnp.concatenate([jnp.broadcast_to(t, (CHUNK, GLA_DK)) for t in tot_rows], axis=0)
    k_dec = (k * jnp.exp(tot - cum)).astype(BF16)

    heads = [(slice(hd * GLA_DKH, (hd + 1) * GLA_DKH), slice(hd * GLA_DVH, (hd + 1) * GLA_DVH))
             for hd in range(GLA_HEADS)]
    for c in range(n_chunks):
        rs = slice(c * CHUNK, (c + 1) * CHUNK)
        for hd, (ks, vs) in enumerate(heads):
            upd_scr[c, hd] = _dot_tn(v[rs, vs], k_dec[rs, ks])
    for c in range(n_chunks):
        dec_row = jnp.exp(tot_rows[c])
        for hd, (ks, vs) in enumerate(heads):
            st = st_scr[hd] * dec_row[:, ks] + upd_scr[c, hd]
            st_scr[hd] = st
            stb_scr[c, hd] = st.astype(BF16)
    for c in range(n_chunks):
        rs = slice(c * CHUNK, (c + 1) * CHUNK)
        for hd, (ks, vs) in enumerate(heads):
            o_scr[rs, vs] = _dot_nt(q[rs, ks], stb_scr[c, hd])

    outs = []
    for hd in range(GLA_HEADS):
        vs = slice(hd * GLA_DVH, (hd + 1) * GLA_DVH)
        o = o_scr[:, vs]
        outs.append(o * lax.rsqrt(jnp.mean(o * o, axis=-1, keepdims=True) + NORM_EPS) * ng_ref[:, vs])
    o = jnp.concatenate(outs, axis=1) * (gate * jax.nn.sigmoid(gate))
    o_ref[...] = x + _dot(o.astype(BF16), wo_ref[...])


def _gla_mixer_layer(x, g, w_in, w_a2, b_a, norm_g, w_o, *, tm):
    L = x.shape[0]
    w_in = w_in.astype(BF16)
    wq = w_in[:, :GLA_DK]
    wk = w_in[:, GLA_DK:2 * GLA_DK]
    wv = w_in[:, 2 * GLA_DK:2 * GLA_DK + GLA_DV]
    wg = w_in[:, 2 * GLA_DK + GLA_DV:2 * GLA_DK + 2 * GLA_DV]
    wa1 = jnp.pad(w_in[:, 2 * GLA_DK + 2 * GLA_DV:], ((0, 0), (0, LANES - GLA_GATE_RANK)))
    wa2 = jnp.pad(w_a2.astype(BF16), ((0, LANES - GLA_GATE_RANK), (0, 0)))
    r = jnp.arange(tm)
    tri = (((r[:, None] // CHUNK) == (r[None, :] // CHUNK)) & (r[None, :] <= r[:, None])).astype(BF16)
    full = _resident
    n_chunks = tm // CHUNK

    row = pl.BlockSpec((tm, D_MODEL), lambda i: (i, 0))
    return pl.pallas_call(
        functools.partial(_gla_kernel, tm=tm),
        grid=(L // tm,),
        in_specs=[row, full((1, D_MODEL)), full((D_MODEL, GLA_DK)), full((D_MODEL, GLA_DK)),
                  full((D_MODEL, GLA_DV)), full((D_MODEL, GLA_DV)), full((D_MODEL, LANES)),
                  full((LANES, GLA_DK)), full((1, GLA_DK)), full((1, GLA_DV)),
                  full((GLA_DV, D_MODEL)), full((tm, tm))],
        out_specs=row,
        out_shape=jax.ShapeDtypeStruct((L, D_MODEL), F32),
        scratch_shapes=[pltpu.VMEM((GLA_HEADS, GLA_DVH, GLA_DKH), F32),
                        pltpu.VMEM((tm, GLA_DV), F32),
                        pltpu.VMEM((n_chunks, GLA_HEADS, GLA_DVH, GLA_DKH), F32),
                        pltpu.VMEM((n_chunks, GLA_HEADS, GLA_DVH, GLA_DKH), BF16)],
        compiler_params=_params("arbitrary"),
        name="gla",
    )(x, g, wq, wk, wv, wg, wa1, wa2, b_a.reshape(1, GLA_DK), norm_g.reshape(1, GLA_DV),
      w_o.astype(BF16), tri)


def _rope_tables(positions):
    half = ROT_DIMS // 2
    inv_freq = ROPE_THETA ** (-jnp.arange(half, dtype=F32) / half)
    ang = positions.astype(F32)[:, None] * inv_freq
    cos = jnp.cos(ang)
    sin = jnp.sin(ang)
    m = jnp.arange(LANES) % DIFF_HD
    cos_l = cos[:, m % half]
    sin_l = sin[:, m % half]
    c = jnp.where(m < ROT_DIMS, cos_l, 1.0)
    s_lo = jnp.where(m < half, -sin_l, 0.0)
    s_hi = jnp.where((m >= half) & (m < ROT_DIMS), sin_l, 0.0)
    return c, s_lo, s_hi


def _qkv_kernel(x_ref, g_ref, w_ref, c_ref, slo_ref, shi_ref, qt_ref, k_ref, vt_ref):
    h = _rms(x_ref[...], g_ref[...]).astype(BF16)
    qkv = _dot(h, w_ref[...])
    c = c_ref[...]
    s_lo = slo_ref[...]
    s_hi = shi_ref[...]
    half = ROT_DIMS // 2

    def rope(t):
        return t * c + pltpu.roll(t, LANES - half, 1) * s_lo + pltpu.roll(t, half, 1) * s_hi

    for j in range(D_MODEL // LANES):
        ls = slice(j * LANES, (j + 1) * LANES)
        q = rope(qkv[:, ls]) * (DIFF_HD ** -0.5 * LOG2E)
        qt_ref[ls, :] = q.T.astype(BF16)
        k_ref[j, 0] = rope(qkv[:, D_MODEL + j * LANES:D_MODEL + (j + 1) * LANES]).astype(BF16)
        vt_ref[j, 0] = qkv[:, 2 * D_MODEL + j * LANES:2 * D_MODEL + (j + 1) * LANES].T.astype(BF16)


def _qkv_layer(x, g, w_qkv, c, s_lo, s_hi, *, tm):
    L = x.shape[0]
    n = L // tm
    row = pl.BlockSpec((tm, D_MODEL), lambda i: (i, 0))
    tab = pl.BlockSpec((tm, LANES), lambda i: (i, 0))
    return pl.pallas_call(
        _qkv_kernel,
        grid=(n,),
        in_specs=[row, pl.BlockSpec((1, D_MODEL), lambda i: (0, 0)),
                  _resident((D_MODEL, 3 * D_MODEL)), tab, tab, tab],
        out_specs=[pl.BlockSpec((D_MODEL, tm), lambda i: (0, i)),
                   pl.BlockSpec((DIFF_HEADS, 1, tm, LANES), lambda i: (0, i, 0, 0)),
                   pl.BlockSpec((DIFF_HEADS, 1, LANES, tm), lambda i: (0, i, 0, 0))],
        out_shape=[jax.ShapeDtypeStruct((D_MODEL, L), BF16),
                   jax.ShapeDtypeStruct((DIFF_HEADS, n, tm, LANES), BF16),
                   jax.ShapeDtypeStruct((DIFF_HEADS, n, LANES, tm), BF16)],
        compiler_params=_params("parallel"),
        name="diff_qkv",
    )(x, g, w_qkv, c, s_lo, s_hi)


def _attn_kernel(qt_ref, k_ref, vt_ref, lamv_ref, sg_ref, bias_ref, o_ref, m_scr, l_scr, acc_scr,
                 sa_scr, sb_scr, *, tq, lambda_init):
    i = pl.program_id(1)
    qt = qt_ref[...]
    row = lax.broadcasted_iota(jnp.int32, qt.shape, 0)
    zero = jnp.zeros_like(qt)
    q_half = (jnp.where(row < DIFF_HD, qt, zero), jnp.where(row >= DIFF_HD, qt, zero))

    m_scr[...] = jnp.full_like(m_scr, NEG_INF)
    l_scr[...] = jnp.zeros_like(l_scr)
    acc_scr[...] = jnp.zeros_like(acc_scr)

    def scores(j, s_ref):
        k = k_ref[0, j]
        for hf in range(2):
            s_ref[hf] = _dot(k, q_half[hf])

    def update(j, s_ref, masked):
        vt = vt_ref[0, j]
        for hf in range(2):
            s = s_ref[hf]
            if masked:
                s = s + bias_ref[...]
            m_prev = m_scr[hf]
            m_new = jnp.maximum(m_prev, jnp.max(s, axis=0, keepdims=True))
            alpha = jnp.exp2(m_prev - m_new)
            p = jnp.exp2(s - m_new)
            l_scr[hf] = alpha * l_scr[hf] + jnp.sum(p, axis=0, keepdims=True)
            acc_scr[hf] = alpha * acc_scr[hf] + _dot(vt, p.astype(BF16))
            m_scr[hf] = m_new

    scores(0, sa_scr)

    def pair(j):
        scores(j + 1, sb_scr)
        update(j, sa_scr, False)
        scores(j + 2, sa_scr)
        update(j + 1, sb_scr, False)

    unroll = 4

    def trip(t, carry):
        for u in range(unroll):
            pair(2 * (unroll * t + u))
        return carry

    n_pairs = i // 2
    lax.fori_loop(0, n_pairs // unroll, trip, 0)
    done = (n_pairs // unroll) * unroll
    for width in (2, 1):
        @pl.when((n_pairs // width) % 2 == 1)
        def _(width=width, done=done):
            for u in range(width):
                pair(2 * (done + u))
        done = done + jnp.where((n_pairs // width) % 2 == 1, width, 0)

    @pl.when(i % 2 == 0)
    def _():
        update(i, sa_scr, True)

    @pl.when(i % 2 == 1)
    def _():
        scores(i, sb_scr)
        update(i - 1, sa_scr, False)
        update(i, sb_scr, True)

    lv = lamv_ref[...]
    lam = (jnp.exp(jnp.sum(lv[0:1] * lv[1:2], axis=-1, keepdims=True))
           - jnp.exp(jnp.sum(lv[2:3] * lv[3:4], axis=-1, keepdims=True)) + lambda_init)
    o = acc_scr[0] / l_scr[0] - lam * (acc_scr[1] / l_scr[1])
    o = o * lax.rsqrt(jnp.mean(o * o, axis=0, keepdims=True) + SUBLN_EPS) * sg_ref[...]
    o_ref[...] = (o * (1.0 - lambda_init)).astype(BF16)


def _attn_layer(qt, k, vt, lamv, subln_g, *, tq, lambda_init):
    n = k.shape[1]
    L = n * tq
    stat = pltpu.VMEM((2, 1, tq), F32)
    chunk = np.arange(tq) // CHUNK
    bias = jnp.asarray(np.where(chunk[:, None] <= chunk[None, :], 0.0, NEG_INF), F32)
    return pl.pallas_call(
        functools.partial(_attn_kernel, tq=tq, lambda_init=lambda_init),
        grid=(DIFF_HEADS, n),
        in_specs=[
            pl.BlockSpec((LANES, tq), lambda p, i: (p, i)),
            pl.BlockSpec((1, n, tq, LANES), lambda p, i: (p, 0, 0, 0)),
            pl.BlockSpec((1, n, LANES, tq), lambda p, i: (p, 0, 0, 0)),
            pl.BlockSpec((4, DIFF_HD), lambda p, i: (0, 0)),
            pl.BlockSpec((LANES, 1), lambda p, i: (0, 0)),
            pl.BlockSpec((tq, tq), lambda p, i: (0, 0), pipeline_mode=pl.Buffered(1)),
        ],
        out_specs=pl.BlockSpec((LANES, tq), lambda p, i: (p, i)),
        out_shape=jax.ShapeDtypeStruct((D_MODEL, L), BF16),
        scratch_shapes=[stat, stat, pltpu.VMEM((2, LANES, tq), F32),
                        pltpu.VMEM((2, tq, tq), F32), pltpu.VMEM((2, tq, tq), F32)],
        compiler_params=_params("parallel", "arbitrary"),
        name="diff_attn",
    )(qt, k, vt, lamv, subln_g, bias)


def _diff_mixer_layer(x, g, positions, w_qkv, lam_q1, lam_k1, lam_q2, lam_k2, subln_g, w_o,
                      lambda_init, *, tq):
    c, s_lo, s_hi = _rope_tables(positions)
    qt, k4, vt4 = _qkv_layer(x, g, w_qkv.astype(BF16), c, s_lo, s_hi, tm=tq)
    lamv = jnp.stack([lam_q1, lam_k1, lam_q2, lam_k2]).astype(F32)
    at = _attn_layer(qt, k4, vt4, lamv, subln_g.reshape(LANES, 1).astype(F32), tq=tq,
                     lambda_init=lambda_init)
    return at, w_o.astype(BF16)


def _trunk(x, positions, norm_mix, norm_ffn, norm_final,
           s5_lam_re, s5_lam_im, s5_log_dt, s5_b_re, s5_b_im, s5_c_re, s5_c_im, s5_d, s5_w_glu,
           gla_w_in, gla_w_a2, gla_b_a, gla_norm, gla_w_o,
           diff_w_qkv, diff_lam_q1, diff_lam_k1, diff_lam_q2, diff_lam_k2, diff_subln, diff_w_o,
           ffn_w_gate_up, ffn_w_down, *, tm, tm_core, tm_gla, tq, tm_ffn, th):
    depth = norm_mix.shape[0]
    gfinal = norm_final.reshape(1, D_MODEL)
    for layer in range(depth):
        kind = layer % N_MIXERS
        idx = layer // N_MIXERS
        g = norm_mix[layer].reshape(1, D_MODEL)
        if kind == 0:
            pre, tm_f = "s5", tm
            pre_args = _s5_mixer_layer(x, g, s5_lam_re[idx], s5_lam_im[idx], s5_log_dt[idx],
                                       s5_b_re[idx], s5_b_im[idx], s5_c_re[idx], s5_c_im[idx],
                                       s5_d[idx], s5_w_glu[idx], tm=tm, tm_core=tm_core)
        elif kind == 1:
            pre, tm_f, pre_args = None, tm_ffn, ()
            x = _gla_mixer_layer(x, g, gla_w_in[idx], gla_w_a2[idx], gla_b_a[idx], gla_norm[idx],
                                 gla_w_o[idx], tm=tm_gla)
        else:
            pre, tm_f = "proj", tm
            lambda_init = 0.8 - 0.6 * math.exp(-0.3 * layer)
            pre_args = _diff_mixer_layer(x, g, positions, diff_w_qkv[idx], diff_lam_q1[idx],
                                         diff_lam_k1[idx], diff_lam_q2[idx], diff_lam_k2[idx],
                                         diff_subln[idx], diff_w_o[idx], lambda_init, tq=tq)
        x = _ffn_layer(x, pre_args, norm_ffn[layer].reshape(1, D_MODEL), ffn_w_gate_up[layer],
                       ffn_w_down[layer], gfinal, final=(layer == depth - 1), tm=tm_f, th=th, pre=pre)
    return x


def kernel(x, positions, norm_mix, norm_ffn, norm_final, s5_lam_re, s5_lam_im, s5_log_dt, s5_b_re, s5_b_im, s5_c_re, s5_c_im, s5_d, s5_w_glu, gla_w_in, gla_w_a2, gla_b_a, gla_norm, gla_w_o, diff_w_qkv, diff_lam_q1, diff_lam_k1, diff_lam_q2, diff_lam_k2, diff_subln, diff_w_o, ffn_w_gate_up, ffn_w_down):
    bsz, seq, _ = x.shape
    outs = []
    for b in range(bsz):
        outs.append(_trunk(
            x[b], positions[b], norm_mix, norm_ffn, norm_final,
            s5_lam_re, s5_lam_im, s5_log_dt, s5_b_re, s5_b_im, s5_c_re, s5_c_im, s5_d, s5_w_glu,
            gla_w_in, gla_w_a2, gla_b_a, gla_norm, gla_w_o,
            diff_w_qkv, diff_lam_q1, diff_lam_k1, diff_lam_q2, diff_lam_k2, diff_subln, diff_w_o,
            ffn_w_gate_up, ffn_w_down,
            tm=512, tm_core=4096, tm_gla=512, tq=512, tm_ffn=1024, th=256))
    return jnp.stack(outs)
```

```python
import functools
import math

import jax
import jax.numpy as jnp
import numpy as np
from jax import lax
from jax.experimental import pallas as pl
from jax.experimental.pallas import tpu as pltpu

F32 = jnp.float32
BF16 = jnp.bfloat16

D_MODEL = 1024
DEPTH = 4
CHUNK = 64
N_MIXERS = 3
NORM_EPS = 1e-6
S5_GROUP = 16
S5_GROUPS = D_MODEL // S5_GROUP
S5_STATE = 64
S5_SUB = 8
S5_GB = 8
GLA_HEADS = 4
GLA_DK = D_MODEL // 2
GLA_DV = D_MODEL
GLA_DKH = GLA_DK // GLA_HEADS
GLA_DVH = GLA_DV // GLA_HEADS
GLA_GATE_RANK = 16
GLA_TEMP = 16.0
DIFF_HD = 64
DIFF_HEADS = D_MODEL // (2 * DIFF_HD)
ROT_DIMS = DIFF_HD // 4
ROPE_THETA = 500000.0
NEG_INF = -1e30
SUBLN_EPS = 1e-5
LOG2E = math.log2(math.e)
FFN_HIDDEN = -(-8 * D_MODEL // (3 * 256)) * 256

LANES = 128
VMEM_LIMIT_BYTES = 56 * 1024 * 1024


def _params(*sem):
    return pltpu.CompilerParams(dimension_semantics=sem, vmem_limit_bytes=VMEM_LIMIT_BYTES)


def _rms(x, g, eps=NORM_EPS):
    return x * lax.rsqrt(jnp.mean(x * x, axis=-1, keepdims=True) + eps) * g


def _dot(a, b):
    return jnp.dot(a, b, preferred_element_type=F32)


def _dot_nt(a, b):
    return lax.dot_general(a, b, (((1,), (1,)), ((), ())), preferred_element_type=F32)


def _dot_tn(a, b):
    return lax.dot_general(a, b, (((0,), (0,)), ((), ())), preferred_element_type=F32)


def _gelu_tanh(x):
    return x * (0.5 * (1.0 + jnp.tanh(math.sqrt(2.0 / math.pi) * (x + 0.044715 * (x * x * x)))))


def _ffn_kernel(*refs, th, final, pre):
    if pre == "s5":
        x_ref, z_ref, wglu_ref, g_ref, wgu_ref, wd_ref, gf_ref, o_ref, h_scr = refs
        ab = _dot(z_ref[...].astype(BF16), wglu_ref[...])
        x = x_ref[...] + ab[:, :D_MODEL] * jax.nn.sigmoid(ab[:, D_MODEL:])
    elif pre == "proj":
        x_ref, at_ref, wo_ref, g_ref, wgu_ref, wd_ref, gf_ref, o_ref, h_scr = refs
        x = x_ref[...] + _dot_tn(at_ref[...], wo_ref[...])
    else:
        x_ref, g_ref, wgu_ref, wd_ref, gf_ref, o_ref, h_scr = refs
        x = x_ref[...]
    h_scr[...] = _rms(x, g_ref[...]).astype(BF16)
    o_ref[...] = x
    for c in range(FFN_HIDDEN // th):
        lo, hi = c * th, (c + 1) * th
        gate = _dot(h_scr[...], wgu_ref[:, lo:hi])
        up = _dot(h_scr[...], wgu_ref[:, FFN_HIDDEN + lo:FFN_HIDDEN + hi])
        act = (gate * jax.nn.sigmoid(gate) * up).astype(BF16)
        o_ref[...] += _dot(act, wd_ref[lo:hi, :])
    if final:
        o_ref[...] = _rms(o_ref[...], gf_ref[...])


def _resident(shape):
    return pl.BlockSpec(shape, lambda i: (0,) * len(shape), pipeline_mode=pl.Buffered(1))


def _ffn_layer(x, pre_args, g, w_gate_up, w_down, gf, *, final, tm, th, pre=None):
    L = x.shape[0]
    row = pl.BlockSpec((tm, D_MODEL), lambda i: (i, 0))
    vec = pl.BlockSpec((1, D_MODEL), lambda i: (0, 0))
    if pre == "s5":
        pre_specs = [row, _resident((D_MODEL, 2 * D_MODEL))]
    elif pre == "proj":
        pre_specs = [pl.BlockSpec((D_MODEL, tm), lambda i: (0, i)), _resident((D_MODEL, D_MODEL))]
    else:
        pre_specs = []
    return pl.pallas_call(
        functools.partial(_ffn_kernel, th=th, final=final, pre=pre),
        grid=(L // tm,),
        in_specs=[row] + pre_specs + [vec, _resident((D_MODEL, 2 * FFN_HIDDEN)),
                                      _resident((FFN_HIDDEN, D_MODEL)), vec],
        out_specs=row,
        out_shape=jax.ShapeDtypeStruct((L, D_MODEL), F32),
        scratch_shapes=[pltpu.VMEM((tm, D_MODEL), BF16)],
        compiler_params=_params("parallel"),
        name="ffn" if pre is None else "ffn_" + pre,
    )(x, *pre_args, g, w_gate_up.astype(BF16), w_down.astype(BF16), gf)


def _s5_tables(lam_re, lam_im, log_dt, b_re, b_im, c_re, c_im):
    G, P, H, T, NB = S5_GROUPS, S5_STATE, S5_GROUP, S5_SUB, S5_GB
    f32 = F32
    lr = lam_re.astype(f32)
    li = lam_im.astype(f32)
    dt = jnp.exp(log_dt.astype(f32))[:, None]
    ab_mag = jnp.exp(lr * dt)
    ab_ang = li * dt
    ab_re = ab_mag * jnp.cos(ab_ang)
    ab_im = ab_mag * jnp.sin(ab_ang)
    den = lr * lr + li * li
    f_re = ((ab_re - 1.0) * lr + ab_im * li) / den
    f_im = (ab_im * lr - (ab_re - 1.0) * li) / den
    br = b_re.astype(f32)
    bi = b_im.astype(f32)
    bb_re = f_re[..., None] * br - f_im[..., None] * bi
    bb_im = f_re[..., None] * bi + f_im[..., None] * br
    cr = c_re.astype(f32)
    ci = c_im.astype(f32)

    def powers(n):
        n = n.astype(f32)[:, None, None]
        mag = jnp.exp(lr * dt * n)
        ang = li * dt * n
        return mag * jnp.cos(ang), mag * jnp.sin(ang)

    pr, pi = powers(jnp.arange(T + 1))
    cp_re = cr[None] * pr[:, :, None, :] - ci[None] * pi[:, :, None, :]
    cp_im = cr[None] * pi[:, :, None, :] + ci[None] * pr[:, :, None, :]
    hp = lax.Precision.HIGHEST
    kern = (jnp.einsum("ngkp,gph->ngkh", cp_re[:T], bb_re, precision=hp)
            - jnp.einsum("ngkp,gph->ngkh", cp_im[:T], bb_im, precision=hp))

    wide = T * NB * H
    t_in = jnp.arange(T)[:, None]
    t_out = jnp.arange(T)[None, :]
    lag = jnp.clip(t_out - t_in, 0, T - 1)
    causal = (t_out >= t_in).astype(f32)
    xk = kern.transpose(1, 0, 3, 2).reshape(NB, NB, T, H, H)
    xk = xk.transpose(0, 2, 1, 3, 4).reshape(NB, T, NB * H, H)
    xk = xk[:, lag] * causal[None, :, :, None, None]
    w = xk.transpose(0, 1, 3, 2, 4).reshape(NB, wide, T * H)

    prq = pr[T - 1 - jnp.arange(T)]
    piq = pi[T - 1 - jnp.arange(T)]
    qv_re = prq[..., None] * bb_re[None] - piq[..., None] * bb_im[None]
    qv_im = prq[..., None] * bb_im[None] + piq[..., None] * bb_re[None]
    qv = jnp.stack([qv_re, qv_im]).reshape(2, T, NB, NB, P, H)
    q = qv.transpose(2, 1, 3, 5, 0, 4).reshape(NB, wide, 2 * P)

    pm = jnp.stack([cp_re[1:], -cp_im[1:]]).reshape(2, T, NB, NB, H, P)
    p = pm.transpose(2, 0, 3, 5, 1, 4).reshape(NB, wide, T * H)
    w, q, p = w.astype(BF16), q.astype(BF16), p.astype(BF16)

    sr, si = powers(T * jnp.arange(1, 9))
    rows = jnp.arange(8)
    kinds = []
    for sh in (1, 2, 4):
        m = (rows >= sh).astype(f32)[:, None, None]
        kinds += [m * sr[sh - 1][None], m * si[sh - 1][None]]
    kinds += [sr, si]
    mult = jnp.stack(kinds)
    mult = mult.reshape(8, 8, NB, NB * P).transpose(2, 0, 1, 3).reshape(NB, 64, NB * P)
    return w, q, p, mult


def _rms_kernel(x_ref, g_ref, o_ref):
    o_ref[...] = _rms(x_ref[...], g_ref[...])


def _rms_layer(x, g, *, tm):
    L = x.shape[0]
    return pl.pallas_call(
        _rms_kernel,
        grid=(L // tm,),
        in_specs=[pl.BlockSpec((tm, D_MODEL), lambda i: (i, 0)),
                  pl.BlockSpec((1, D_MODEL), lambda i: (0, 0))],
        out_specs=pl.BlockSpec((tm, D_MODEL), lambda i: (i, 0)),
        out_shape=jax.ShapeDtypeStruct((L, D_MODEL), F32),
        compiler_params=_params("parallel"),
        name="s5_norm",
    )(x, g)


def _s5_expanders():
    T, NB, H, P = S5_SUB, S5_GB, S5_GROUP, S5_STATE
    wide = T * NB * H
    r = np.arange(LANES)[:, None]
    c = np.arange(wide)[None, :]
    rr = np.arange(wide)[:, None]
    e_out = (r // H == c // (NB * H)) & (r % H == c % H)
    e_state = (r // P == c // (NB * P)) & (r % P == c % P)
    gl_io_r, gl_st_r = (rr // H) % NB, (rr // P) % NB
    gl_io_c, gl_st_c = (c // H) % NB, (c // P) % NB
    consts = (e_out, e_state, gl_io_r == gl_io_c, gl_io_r == gl_st_c, gl_st_r == gl_io_c)
    return [jnp.asarray(a, BF16) for a in consts]


def _s5_core_kernel(h_ref, wc_ref, qc_ref, pc_ref, m_ref, d_ref, eo_ref, es_ref, mw_ref, mq_ref, mp_ref,
                    z_ref, carry_scr, b_scr, s_scr, w_scr, q_scr, p_scr, *, rows):
    half = S5_GB * S5_STATE
    strip = 2 * LANES

    @pl.when(pl.program_id(1) == 0)
    def _():
        carry_scr[...] = jnp.zeros_like(carry_scr)
        for c_ref, e_ref, k_ref, dst in ((wc_ref, eo_ref, mw_ref, w_scr), (qc_ref, es_ref, mq_ref, q_scr),
                                         (pc_ref, eo_ref, mp_ref, p_scr)):
            for c0 in range(0, S5_SUB * LANES, strip):
                cs = slice(c0, c0 + strip)
                dst[:, cs] = (_dot(c_ref[0], e_ref[:, cs]) * k_ref[:, cs]).astype(BF16)

    h_t = [h_ref[pl.ds(t, rows, stride=S5_SUB), :] for t in range(S5_SUB)]
    x = jnp.concatenate(h_t, axis=1).astype(BF16)
    b_scr[...] = _dot(x, q_scr[...])

    first_row = lax.broadcasted_iota(jnp.int32, (8, half), 0) == 0
    c_re = carry_scr[:, :half]
    c_im = carry_scr[:, half:]
    for tile in range(rows // 8):
        r0 = tile * 8
        x_re = b_scr[r0:r0 + 8, :half]
        x_im = b_scr[r0:r0 + 8, half:]
        for lvl, sh in enumerate((1, 2, 4)):
            a_re = m_ref[0, 16 * lvl:16 * lvl + 8, :]
            a_im = m_ref[0, 16 * lvl + 8:16 * lvl + 16, :]
            r_re = pltpu.roll(x_re, sh, 0)
            r_im = pltpu.roll(x_im, sh, 0)
            x_re, x_im = (x_re + a_re * r_re - a_im * r_im,
                          x_im + a_re * r_im + a_im * r_re)
        p_re = m_ref[0, 48:56, :]
        p_im = m_ref[0, 56:64, :]
        s_re = x_re + p_re * c_re - p_im * c_im
        s_im = x_im + p_re * c_im + p_im * c_re
        s_scr[r0:r0 + 8, :half] = jnp.where(first_row, c_re, pltpu.roll(s_re, 1, 0))
        s_scr[r0:r0 + 8, half:] = jnp.where(first_row, c_im, pltpu.roll(s_im, 1, 0))
        c_re = jnp.broadcast_to(s_re[7:8, :], (8, half))
        c_im = jnp.broadcast_to(s_im[7:8, :], (8, half))
    carry_scr[:, :half] = c_re
    carry_scr[:, half:] = c_im

    s = s_scr[...].astype(BF16)
    for c0 in range(0, S5_SUB * LANES, strip):
        y = _dot(x[:, :c0 + strip], w_scr[:c0 + strip, c0:c0 + strip]) + _dot(s, p_scr[:, c0:c0 + strip])
        for t in range(c0 // LANES, (c0 + strip) // LANES):
            y_t = y[:, t * LANES - c0:(t + 1) * LANES - c0] + d_ref[...] * h_t[t]
            z_ref[pl.ds(t, rows, stride=S5_SUB), :] = _gelu_tanh(y_t)


def _s5_core(h, w, q, p, mult, d_skip, *, tm):
    L = h.shape[0]
    rows = tm // S5_SUB
    wide = S5_SUB * LANES
    compact = pl.BlockSpec((1, wide, LANES), lambda g, i: (g, 0, 0))
    expand = pl.BlockSpec((LANES, wide), lambda g, i: (0, 0), pipeline_mode=pl.Buffered(1))
    mask = pl.BlockSpec((wide, wide), lambda g, i: (0, 0), pipeline_mode=pl.Buffered(1))
    table = pltpu.VMEM((wide, wide), BF16)
    return pl.pallas_call(
        functools.partial(_s5_core_kernel, rows=rows),
        grid=(S5_GB, L // tm),
        in_specs=[
            pl.BlockSpec((tm, LANES), lambda g, i: (i, g)),
            compact, compact, compact,
            pl.BlockSpec((1, 64, S5_GB * S5_STATE), lambda g, i: (g, 0, 0)),
            pl.BlockSpec((1, LANES), lambda g, i: (0, g)),
            expand, expand, mask, mask, mask,
        ],
        out_specs=pl.BlockSpec((tm, LANES), lambda g, i: (i, g)),
        out_shape=jax.ShapeDtypeStruct((L, D_MODEL), F32),
        scratch_shapes=[pltpu.VMEM((8, wide), F32), pltpu.VMEM((rows, wide), F32),
                        pltpu.VMEM((rows, wide), F32), table, table, table],
        compiler_params=_params("parallel", "arbitrary"),
        name="s5_core",
    )(h, w, q, p, mult, d_skip, *_s5_expanders())


def _s5_mixer_layer(x, g, lam_re, lam_im, log_dt, b_re, b_im, c_re, c_im, d_skip, w_glu, *, tm, tm_core):
    w, q, p, mult = _s5_tables(lam_re, lam_im, log_dt, b_re, b_im, c_re, c_im)
    h = _rms_layer(x, g, tm=tm)
    z = _s5_core(h, w, q, p, mult, d_skip.reshape(1, D_MODEL), tm=tm_core)
    return z, w_glu.astype(BF16)


def _gla_kernel(x_ref, g_ref, wq_ref, wk_ref, wv_ref, wg_ref, wa1_ref, wa2_ref, ba_ref, ng_ref,
                wo_ref, tri_ref, o_ref, st_scr, o_scr, upd_scr, stb_scr, *, tm):
    @pl.when(pl.program_id(0) == 0)
    def _():
        st_scr[...] = jnp.zeros_like(st_scr)

    x = x_ref[...]
    h = _rms(x, g_ref[...]).astype(BF16)
    q = (_dot(h, wq_ref[...]) * (GLA_DKH ** -0.5)).astype(BF16)
    k = _dot(h, wk_ref[...])
    v = _dot(h, wv_ref[...]).astype(BF16)
    gate = _dot(h, wg_ref[...])
    a_lo = _dot(h, wa1_ref[...]).astype(BF16)
    log_a = jax.nn.log_sigmoid(_dot(a_lo, wa2_ref[...]) + ba_ref[...]) / GLA_TEMP

    la_hi = log_a.astype(BF16)
    la_lo = (log_a - la_hi.astype(F32)).astype(BF16)
    cum = _dot(tri_ref[...], la_hi) + _dot(tri_ref[...], la_lo)
    n_chunks = tm // CHUNK
    tot_rows = [cum[(c + 1) * CHUNK - 1:(c + 1) * CHUNK, :] for c in range(n_chunks)]
    tot = jnp.concatenate([jnp.broadcast_to(t, (CHUNK, GLA_DK)) for t in tot_rows], axis=0)
    k_dec = (k * jnp.exp(tot - cum)).astype(BF16)

    heads = [(slice(hd * GLA_DKH, (hd + 1) * GLA_DKH), slice(hd * GLA_DVH, (hd + 1) * GLA_DVH))
             for hd in range(GLA_HEADS)]
    for c in range(n_chunks):
        rs = slice(c * CHUNK, (c + 1) * CHUNK)
        for hd, (ks, vs) in enumerate(heads):
            upd_scr[c, hd] = _dot_tn(v[rs, vs], k_dec[rs, ks])
    for c in range(n_chunks):
        dec_row = jnp.exp(tot_rows[c])
        for hd, (ks, vs) in enumerate(heads):
            st = st_scr[hd] * dec_row[:, ks] + upd_scr[c, hd]
            st_scr[hd] = st
            stb_scr[c, hd] = st.astype(BF16)
    for c in range(n_chunks):
        rs = slice(c * CHUNK, (c + 1) * CHUNK)
        for hd, (ks, vs) in enumerate(heads):
            o_scr[rs, vs] = _dot_nt(q[rs, ks], stb_scr[c, hd])

    outs = []
    for hd in range(GLA_HEADS):
        vs = slice(hd * GLA_DVH, (hd + 1) * GLA_DVH)
        o = o_scr[:, vs]
        outs.append(o * lax.rsqrt(jnp.mean(o * o, axis=-1, keepdims=True) + NORM_EPS) * ng_ref[:, vs])
    o = jnp.concatenate(outs, axis=1) * (gate * jax.nn.sigmoid(gate))
    o_ref[...] = x + _dot(o.astype(BF16), wo_ref[...])


def _gla_mixer_layer(x, g, w_in, w_a2, b_a, norm_g, w_o, *, tm):
    L = x.shape[0]
    w_in = w_in.astype(BF16)
    wq = w_in[:, :GLA_DK]
    wk = w_in[:, GLA_DK:2 * GLA_DK]
    wv = w_in[:, 2 * GLA_DK:2 * GLA_DK + GLA_DV]
    wg = w_in[:, 2 * GLA_DK + GLA_DV:2 * GLA_DK + 2 * GLA_DV]
    wa1 = jnp.pad(w_in[:, 2 * GLA_DK + 2 * GLA_DV:], ((0, 0), (0, LANES - GLA_GATE_RANK)))
    wa2 = jnp.pad(w_a2.astype(BF16), ((0, LANES - GLA_GATE_RANK), (0, 0)))
    r = jnp.arange(tm)
    tri = (((r[:, None] // CHUNK) == (r[None, :] // CHUNK)) & (r[None, :] <= r[:, None])).astype(BF16)
    full = _resident
    n_chunks = tm // CHUNK

    row = pl.BlockSpec((tm, D_MODEL), lambda i: (i, 0))
    return pl.pallas_call(
        functools.partial(_gla_kernel, tm=tm),
        grid=(L // tm,),
        in_specs=[row, full((1, D_MODEL)), full((D_MODEL, GLA_DK)), full((D_MODEL, GLA_DK)),
                  full((D_MODEL, GLA_DV)), full((D_MODEL, GLA_DV)), full((D_MODEL, LANES)),
                  full((LANES, GLA_DK)), full((1, GLA_DK)), full((1, GLA_DV)),
                  full((GLA_DV, D_MODEL)), full((tm, tm))],
        out_specs=row,
        out_shape=jax.ShapeDtypeStruct((L, D_MODEL), F32),
        scratch_shapes=[pltpu.VMEM((GLA_HEADS, GLA_DVH, GLA_DKH), F32),
                        pltpu.VMEM((tm, GLA_DV), F32),
                        pltpu.VMEM((n_chunks, GLA_HEADS, GLA_DVH, GLA_DKH), F32),
                        pltpu.VMEM((n_chunks, GLA_HEADS, GLA_DVH, GLA_DKH), BF16)],
        compiler_params=_params("arbitrary"),
        name="gla",
    )(x, g, wq, wk, wv, wg, wa1, wa2, b_a.reshape(1, GLA_DK), norm_g.reshape(1, GLA_DV),
      w_o.astype(BF16), tri)


def _rope_tables(positions):
    half = ROT_DIMS // 2
    inv_freq = ROPE_THETA ** (-jnp.arange(half, dtype=F32) / half)
    ang = positions.astype(F32)[:, None] * inv_freq
    cos = jnp.cos(ang)
    sin = jnp.sin(ang)
    m = jnp.arange(LANES) % DIFF_HD
    cos_l = cos[:, m % half]
    sin_l = sin[:, m % half]
    c = jnp.where(m < ROT_DIMS, cos_l, 1.0)
    s_lo = jnp.where(m < half, -sin_l, 0.0)
    s_hi = jnp.where((m >= half) & (m < ROT_DIMS), sin_l, 0.0)
    return c, s_lo, s_hi


def _qkv_kernel(x_ref, g_ref, w_ref, c_ref, slo_ref, shi_ref, qt_ref, k_ref, vt_ref):
    h = _rms(x_ref[...], g_ref[...]).astype(BF16)
    qkv = _dot(h, w_ref[...])
    c = c_ref[...]
    s_lo = slo_ref[...]
    s_hi = shi_ref[...]
    half = ROT_DIMS // 2

    def rope(t):
        return t * c + pltpu.roll(t, LANES - half, 1) * s_lo + pltpu.roll(t, half, 1) * s_hi

    for j in range(D_MODEL // LANES):
        ls = slice(j * LANES, (j + 1) * LANES)
        q = rope(qkv[:, ls]) * (DIFF_HD ** -0.5 * LOG2E)
        qt_ref[ls, :] = q.T.astype(BF16)
        k_ref[j, 0] = rope(qkv[:, D_MODEL + j * LANES:D_MODEL + (j + 1) * LANES]).astype(BF16)
        vt_ref[j, 0] = qkv[:, 2 * D_MODEL + j * LANES:2 * D_MODEL + (j + 1) * LANES].T.astype(BF16)


def _qkv_layer(x, g, w_qkv, c, s_lo, s_hi, *, tm):
    L = x.shape[0]
    n = L // tm
    row = pl.BlockSpec((tm, D_MODEL), lambda i: (i, 0))
    tab = pl.BlockSpec((tm, LANES), lambda i: (i, 0))
    return pl.pallas_call(
        _qkv_kernel,
        grid=(n,),
        in_specs=[row, pl.BlockSpec((1, D_MODEL), lambda i: (0, 0)),
                  _resident((D_MODEL, 3 * D_MODEL)), tab, tab, tab],
        out_specs=[pl.BlockSpec((D_MODEL, tm), lambda i: (0, i)),
                   pl.BlockSpec((DIFF_HEADS, 1, tm, LANES), lambda i: (0, i, 0, 0)),
                   pl.BlockSpec((DIFF_HEADS, 1, LANES, tm), lambda i: (0, i, 0, 0))],
        out_shape=[jax.ShapeDtypeStruct((D_MODEL, L), BF16),
                   jax.ShapeDtypeStruct((DIFF_HEADS, n, tm, LANES), BF16),
                   jax.ShapeDtypeStruct((DIFF_HEADS, n, LANES, tm), BF16)],
        compiler_params=_params("parallel"),
        name="diff_qkv",
    )(x, g, w_qkv, c, s_lo, s_hi)


def _attn_kernel(qt_ref, k_ref, vt_ref, lamv_ref, sg_ref, bias_ref, o_ref, m_scr, l_scr, acc_scr,
                 sa_scr, sb_scr, ma_scr, mb_scr, *, ratio, lambda_init):
    qi = pl.program_id(1)
    i = qi // ratio
    qt = qt_ref[...]
    row = lax.broadcasted_iota(jnp.int32, qt.shape, 0)
    zero = jnp.zeros_like(qt)
    q_half = (jnp.where(row < DIFF_HD, qt, zero), jnp.where(row >= DIFF_HD, qt, zero))

    m_scr[...] = jnp.full_like(m_scr, NEG_INF)
    l_scr[...] = jnp.zeros_like(l_scr)
    acc_scr[...] = jnp.zeros_like(acc_scr)

    def scores(j, slot, masked=False):
        s_ref, smax_ref = slot
        k = k_ref[0, j]
        for hf in range(2):
            s = _dot(k, q_half[hf])
            if masked:
                s = s + bias_ref[qi % ratio]
            s_ref[hf] = s
            smax_ref[hf] = jnp.max(s, axis=0, keepdims=True)

    def update(j, slot):
        s_ref, smax_ref = slot
        vt = vt_ref[0, j]
        for hf in range(2):
            m_prev = m_scr[hf]
            m_new = jnp.maximum(m_prev, smax_ref[hf])
            alpha = jnp.exp2(m_prev - m_new)
            p = jnp.exp2(s_ref[hf] - m_new)
            l_scr[hf] = alpha * l_scr[hf] + jnp.sum(p, axis=0, keepdims=True)
            acc_scr[hf] = alpha * acc_scr[hf] + _dot(vt, p.astype(BF16))
            m_scr[hf] = m_new

    slot_a, slot_b = (sa_scr, ma_scr), (sb_scr, mb_scr)

    def pair(j):
        scores(j + 1, slot_b)
        update(j, slot_a)
        scores(j + 2, slot_a)
        update(j + 1, slot_b)

    unroll = 4

    def trip(t, carry):
        for u in range(unroll):
            pair(2 * (unroll * t + u))
        return carry

    @pl.when(i == 0)
    def _():
        scores(0, slot_a, masked=True)
        update(0, slot_a)

    @pl.when(i > 0)
    def _():
        scores(0, slot_a)

    n_pairs = jnp.maximum(i - 1, 0) // 2
    lax.fori_loop(0, n_pairs // unroll, trip, 0)
    done = (n_pairs // unroll) * unroll
    for width in (2, 1):
        @pl.when((n_pairs // width) % 2 == 1)
        def _(width=width, done=done):
            for u in range(width):
                pair(2 * (done + u))
        done = done + jnp.where((n_pairs // width) % 2 == 1, width, 0)

    @pl.when(i % 2 == 1)
    def _():
        scores(i, slot_b, masked=True)
        update(i - 1, slot_a)
        update(i, slot_b)

    @pl.when((i % 2 == 0) & (i > 0))
    def _():
        scores(i - 1, slot_b)
        update(i - 2, slot_a)
        scores(i, slot_a, masked=True)
        update(i - 1, slot_b)
        update(i, slot_a)

    lv = lamv_ref[...]
    lam = (jnp.exp(jnp.sum(lv[0:1] * lv[1:2], axis=-1, keepdims=True))
           - jnp.exp(jnp.sum(lv[2:3] * lv[3:4], axis=-1, keepdims=True)) + lambda_init)
    o = acc_scr[0] / l_scr[0] - lam * (acc_scr[1] / l_scr[1])
    o = o * lax.rsqrt(jnp.mean(o * o, axis=0, keepdims=True) + SUBLN_EPS) * sg_ref[...]
    o_ref[...] = (o * (1.0 - lambda_init)).astype(BF16)


def _attn_layer(qt, k, vt, lamv, subln_g, *, tq, lambda_init):
    n, tk = k.shape[1], k.shape[2]
    L = n * tk
    ratio = tk // tq
    stat = pltpu.VMEM((2, 1, tq), F32)
    k_chunk = np.arange(tk)[:, None] // CHUNK
    q_chunk = np.arange(tq)[None, :] // CHUNK
    bias = np.stack([np.where(k_chunk <= r * (tq // CHUNK) + q_chunk, 0.0, NEG_INF) for r in range(ratio)])
    return pl.pallas_call(
        functools.partial(_attn_kernel, ratio=ratio, lambda_init=lambda_init),
        grid=(DIFF_HEADS, L // tq),
        in_specs=[
            pl.BlockSpec((LANES, tq), lambda p, i: (p, i)),
            pl.BlockSpec((1, n, tk, LANES), lambda p, i: (p, 0, 0, 0)),
            pl.BlockSpec((1, n, LANES, tk), lambda p, i: (p, 0, 0, 0)),
            pl.BlockSpec((4, DIFF_HD), lambda p, i: (0, 0)),
            pl.BlockSpec((LANES, 1), lambda p, i: (0, 0)),
            pl.BlockSpec((ratio, tk, tq), lambda p, i: (0, 0, 0), pipeline_mode=pl.Buffered(1)),
        ],
        out_specs=pl.BlockSpec((LANES, tq), lambda p, i: (p, i)),
        out_shape=jax.ShapeDtypeStruct((D_MODEL, L), BF16),
        scratch_shapes=[stat, stat, pltpu.VMEM((2, LANES, tq), F32),
                        pltpu.VMEM((2, tk, tq), F32), pltpu.VMEM((2, tk, tq), F32), stat, stat],
        compiler_params=_params("parallel", "arbitrary"),
        name="diff_attn",
    )(qt, k, vt, lamv, subln_g, jnp.asarray(bias, F32))


def _diff_mixer_layer(x, g, positions, w_qkv, lam_q1, lam_k1, lam_q2, lam_k2, subln_g, w_o,
                      lambda_init, *, tq, tk):
    c, s_lo, s_hi = _rope_tables(positions)
    qt, k4, vt4 = _qkv_layer(x, g, w_qkv.astype(BF16), c, s_lo, s_hi, tm=tk)
    lamv = jnp.stack([lam_q1, lam_k1, lam_q2, lam_k2]).astype(F32)
    at = _attn_layer(qt, k4, vt4, lamv, subln_g.reshape(LANES, 1).astype(F32), tq=tq,
                     lambda_init=lambda_init)
    return at, w_o.astype(BF16)


def _trunk(x, positions, norm_mix, norm_ffn, norm_final,
           s5_lam_re, s5_lam_im, s5_log_dt, s5_b_re, s5_b_im, s5_c_re, s5_c_im, s5_d, s5_w_glu,
           gla_w_in, gla_w_a2, gla_b_a, gla_norm, gla_w_o,
           diff_w_qkv, diff_lam_q1, diff_lam_k1, diff_lam_q2, diff_lam_k2, diff_subln, diff_w_o,
           ffn_w_gate_up, ffn_w_down, *, tm, tm_core, tm_gla, tq, tk, tm_ffn, th):
    depth = norm_mix.shape[0]
    gfinal = norm_final.reshape(1, D_MODEL)
    for layer in range(depth):
        kind = layer % N_MIXERS
        idx = layer // N_MIXERS
        g = norm_mix[layer].reshape(1, D_MODEL)
        if kind == 0:
            pre, tm_f = "s5", tm
            pre_args = _s5_mixer_layer(x, g, s5_lam_re[idx], s5_lam_im[idx], s5_log_dt[idx],
                                       s5_b_re[idx], s5_b_im[idx], s5_c_re[idx], s5_c_im[idx],
                                       s5_d[idx], s5_w_glu[idx], tm=tm, tm_core=tm_core)
        elif kind == 1:
            pre, tm_f, pre_args = None, tm_ffn, ()
            x = _gla_mixer_layer(x, g, gla_w_in[idx], gla_w_a2[idx], gla_b_a[idx], gla_norm[idx],
                                 gla_w_o[idx], tm=tm_gla)
        else:
            pre, tm_f = "proj", tm
            lambda_init = 0.8 - 0.6 * math.exp(-0.3 * layer)
            pre_args = _diff_mixer_layer(x, g, positions, diff_w_qkv[idx], diff_lam_q1[idx],
                                         diff_lam_k1[idx], diff_lam_q2[idx], diff_lam_k2[idx],
                                         diff_subln[idx], diff_w_o[idx], lambda_init, tq=tq, tk=tk)
        x = _ffn_layer(x, pre_args, norm_ffn[layer].reshape(1, D_MODEL), ffn_w_gate_up[layer],
                       ffn_w_down[layer], gfinal, final=(layer == depth - 1), tm=tm_f, th=th, pre=pre)
    return x


def kernel(x, positions, norm_mix, norm_ffn, norm_final, s5_lam_re, s5_lam_im, s5_log_dt, s5_b_re, s5_b_im, s5_c_re, s5_c_im, s5_d, s5_w_glu, gla_w_in, gla_w_a2, gla_b_a, gla_norm, gla_w_o, diff_w_qkv, diff_lam_q1, diff_lam_k1, diff_lam_q2, diff_lam_k2, diff_subln, diff_w_o, ffn_w_gate_up, ffn_w_down):
    bsz, seq, _ = x.shape
    outs = []
    for b in range(bsz):
        outs.append(_trunk(
            x[b], positions[b], norm_mix, norm_ffn, norm_final,
            s5_lam_re, s5_lam_im, s5_log_dt, s5_b_re, s5_b_im, s5_c_re, s5_c_im, s5_d, s5_w_glu,
            gla_w_in, gla_w_a2, gla_b_a, gla_norm, gla_w_o,
            diff_w_qkv, diff_lam_q1, diff_lam_k1, diff_lam_q2, diff_lam_k2, diff_subln, diff_w_o,
            ffn_w_gate_up, ffn_w_down,
            tm=512, tm_core=4096, tm_gla=512, tq=512, tk=1024, tm_ffn=1024, th=256))
    return jnp.stack(outs)
```

```python
import functools
import math

import jax
import jax.numpy as jnp
import numpy as np
from jax import lax
from jax.experimental import pallas as pl
from jax.experimental.pallas import tpu as pltpu

F32 = jnp.float32
BF16 = jnp.bfloat16

D_MODEL = 1024
DEPTH = 4
CHUNK = 64
N_MIXERS = 3
NORM_EPS = 1e-6
S5_GROUP = 16
S5_GROUPS = D_MODEL // S5_GROUP
S5_STATE = 64
S5_SUB = 8
S5_GB = 8
GLA_HEADS = 4
GLA_DK = D_MODEL // 2
GLA_DV = D_MODEL
GLA_DKH = GLA_DK // GLA_HEADS
GLA_DVH = GLA_DV // GLA_HEADS
GLA_GATE_RANK = 16
GLA_TEMP = 16.0
DIFF_HD = 64
DIFF_HEADS = D_MODEL // (2 * DIFF_HD)
ROT_DIMS = DIFF_HD // 4
ROPE_THETA = 500000.0
NEG_INF = -1e30
SUBLN_EPS = 1e-5
LOG2E = math.log2(math.e)
FFN_HIDDEN = -(-8 * D_MODEL // (3 * 256)) * 256

LANES = 128
VMEM_LIMIT_BYTES = 56 * 1024 * 1024


def _params(*sem):
    return pltpu.CompilerParams(dimension_semantics=sem, vmem_limit_bytes=VMEM_LIMIT_BYTES)


def _rms(x, g, eps=NORM_EPS):
    return x * lax.rsqrt(jnp.mean(x * x, axis=-1, keepdims=True) + eps) * g


def _dot(a, b):
    return jnp.dot(a, b, preferred_element_type=F32)


def _dot_nt(a, b):
    return lax.dot_general(a, b, (((1,), (1,)), ((), ())), preferred_element_type=F32)


def _dot_tn(a, b):
    return lax.dot_general(a, b, (((0,), (0,)), ((), ())), preferred_element_type=F32)


def _cast_kernel(w_ref, o_ref):
    o_ref[...] = w_ref[...].astype(BF16)


def _to_bf16(w, cols=None):
    n_layers, rows, full = w.shape
    cols = full if cols is None else cols
    row_blocks = 4
    spec = pl.BlockSpec((1, rows // row_blocks, cols), lambda l, r: (l, r, 0))
    return pl.pallas_call(
        _cast_kernel,
        grid=(n_layers, row_blocks),
        in_specs=[spec],
        out_specs=spec,
        out_shape=jax.ShapeDtypeStruct((n_layers, rows, cols), BF16),
        compiler_params=_params("parallel", "parallel"),
        name="to_bf16",
    )(w)


def _gelu_tanh(x):
    return x * (0.5 * (1.0 + jnp.tanh(math.sqrt(2.0 / math.pi) * (x + 0.044715 * (x * x * x)))))


def _ffn_kernel(*refs, th, final, pre):
    if pre == "s5":
        x_ref, z_ref, wglu_ref, g_ref, wgu_ref, wd_ref, gf_ref, o_ref, h_scr = refs
        ab = _dot(z_ref[...].astype(BF16), wglu_ref[...])
        x = x_ref[...] + ab[:, :D_MODEL] * jax.nn.sigmoid(ab[:, D_MODEL:])
    elif pre == "proj":
        x_ref, at_ref, wo_ref, g_ref, wgu_ref, wd_ref, gf_ref, o_ref, h_scr = refs
        x = x_ref[...] + _dot_tn(at_ref[...], wo_ref[...])
    else:
        x_ref, g_ref, wgu_ref, wd_ref, gf_ref, o_ref, h_scr = refs
        x = x_ref[...]
    h_scr[...] = _rms(x, g_ref[...]).astype(BF16)
    o_ref[...] = x
    for c in range(FFN_HIDDEN // th):
        lo, hi = c * th, (c + 1) * th
        gate = _dot(h_scr[...], wgu_ref[0, :, lo:hi])
        up = _dot(h_scr[...], wgu_ref[0, :, FFN_HIDDEN + lo:FFN_HIDDEN + hi])
        act = (gate * jax.nn.sigmoid(gate) * up).astype(BF16)
        o_ref[...] += _dot(act, wd_ref[0, lo:hi, :])
    if final:
        o_ref[...] = _rms(o_ref[...], gf_ref[...])


def _resident(shape):
    return pl.BlockSpec(shape, lambda i: (0,) * len(shape), pipeline_mode=pl.Buffered(1))


def _ffn_layer(x, pre_args, g, w_gate_up, w_down, gf, *, layer, final, tm, th, pre=None):
    L = x.shape[0]
    row = pl.BlockSpec((tm, D_MODEL), lambda i: (i, 0))
    vec = pl.BlockSpec((1, D_MODEL), lambda i: (0, 0))

    def slab(rows, cols):
        return pl.BlockSpec((1, rows, cols), lambda i: (layer, 0, 0), pipeline_mode=pl.Buffered(1))

    if pre == "s5":
        pre_specs = [row, _resident((D_MODEL, 2 * D_MODEL))]
    elif pre == "proj":
        pre_specs = [pl.BlockSpec((D_MODEL, tm), lambda i: (0, i)), _resident((D_MODEL, D_MODEL))]
    else:
        pre_specs = []
    return pl.pallas_call(
        functools.partial(_ffn_kernel, th=th, final=final, pre=pre),
        grid=(L // tm,),
        in_specs=[row] + pre_specs + [vec, slab(D_MODEL, 2 * FFN_HIDDEN), slab(FFN_HIDDEN, D_MODEL), vec],
        out_specs=row,
        out_shape=jax.ShapeDtypeStruct((L, D_MODEL), F32),
        scratch_shapes=[pltpu.VMEM((tm, D_MODEL), BF16)],
        compiler_params=_params("parallel"),
        name="ffn" if pre is None else "ffn_" + pre,
    )(x, *pre_args, g, w_gate_up, w_down, gf)


def _s5_tables(lam_re, lam_im, log_dt, b_re, b_im, c_re, c_im):
    G, P, H, T, NB = S5_GROUPS, S5_STATE, S5_GROUP, S5_SUB, S5_GB
    f32 = F32
    lr = lam_re.astype(f32)
    li = lam_im.astype(f32)
    dt = jnp.exp(log_dt.astype(f32))[:, None]
    ab_mag = jnp.exp(lr * dt)
    ab_ang = li * dt
    ab_re = ab_mag * jnp.cos(ab_ang)
    ab_im = ab_mag * jnp.sin(ab_ang)
    den = lr * lr + li * li
    f_re = ((ab_re - 1.0) * lr + ab_im * li) / den
    f_im = (ab_im * lr - (ab_re - 1.0) * li) / den
    br = b_re.astype(f32)
    bi = b_im.astype(f32)
    bb_re = f_re[..., None] * br - f_im[..., None] * bi
    bb_im = f_re[..., None] * bi + f_im[..., None] * br
    cr = c_re.astype(f32)
    ci = c_im.astype(f32)

    def powers(n):
        n = n.astype(f32)[:, None, None]
        mag = jnp.exp(lr * dt * n)
        ang = li * dt * n
        return mag * jnp.cos(ang), mag * jnp.sin(ang)

    pr, pi = powers(jnp.arange(T + 1))
    cp_re = cr[None] * pr[:, :, None, :] - ci[None] * pi[:, :, None, :]
    cp_im = cr[None] * pi[:, :, None, :] + ci[None] * pr[:, :, None, :]
    hp = lax.Precision.HIGHEST
    kern = (jnp.einsum("ngkp,gph->ngkh", cp_re[:T], bb_re, precision=hp)
            - jnp.einsum("ngkp,gph->ngkh", cp_im[:T], bb_im, precision=hp))

    wide = T * NB * H
    t_in = jnp.arange(T)[:, None]
    t_out = jnp.arange(T)[None, :]
    lag = jnp.clip(t_out - t_in, 0, T - 1)
    causal = (t_out >= t_in).astype(f32)
    xk = kern.transpose(1, 0, 3, 2).reshape(NB, NB, T, H, H)
    xk = xk.transpose(0, 2, 1, 3, 4).reshape(NB, T, NB * H, H)
    xk = xk[:, lag] * causal[None, :, :, None, None]
    w = xk.transpose(0, 1, 3, 2, 4).reshape(NB, wide, T * H)

    prq = pr[T - 1 - jnp.arange(T)]
    piq = pi[T - 1 - jnp.arange(T)]
    qv_re = prq[..., None] * bb_re[None] - piq[..., None] * bb_im[None]
    qv_im = prq[..., None] * bb_im[None] + piq[..., None] * bb_re[None]
    qv = jnp.stack([qv_re, qv_im]).reshape(2, T, NB, NB, P, H)
    q = qv.transpose(2, 1, 3, 5, 0, 4).reshape(NB, wide, 2 * P)

    pm = jnp.stack([cp_re[1:], -cp_im[1:]]).reshape(2, T, NB, NB, H, P)
    p = pm.transpose(2, 0, 3, 5, 1, 4).reshape(NB, wide, T * H)
    w, q, p = w.astype(BF16), q.astype(BF16), p.astype(BF16)

    sr, si = powers(T * jnp.arange(1, 9))
    rows = jnp.arange(8)
    kinds = []
    for sh in (1, 2, 4):
        m = (rows >= sh).astype(f32)[:, None, None]
        kinds += [m * sr[sh - 1][None], m * si[sh - 1][None]]
    kinds += [sr, si]
    mult = jnp.stack(kinds)
    mult = mult.reshape(8, 8, NB, NB * P).transpose(2, 0, 1, 3).reshape(NB, 64, NB * P)
    return w, q, p, mult


def _rms_kernel(x_ref, g_ref, o_ref):
    o_ref[...] = _rms(x_ref[...], g_ref[...])


def _rms_layer(x, g, *, tm):
    L = x.shape[0]
    return pl.pallas_call(
        _rms_kernel,
        grid=(L // tm,),
        in_specs=[pl.BlockSpec((tm, D_MODEL), lambda i: (i, 0)),
                  pl.BlockSpec((1, D_MODEL), lambda i: (0, 0))],
        out_specs=pl.BlockSpec((tm, D_MODEL), lambda i: (i, 0)),
        out_shape=jax.ShapeDtypeStruct((L, D_MODEL), F32),
        compiler_params=_params("parallel"),
        name="s5_norm",
    )(x, g)


def _s5_expanders():
    T, NB, H, P = S5_SUB, S5_GB, S5_GROUP, S5_STATE
    wide = T * NB * H
    r = np.arange(LANES)[:, None]
    c = np.arange(wide)[None, :]
    rr = np.arange(wide)[:, None]
    e_out = (r // H == c // (NB * H)) & (r % H == c % H)
    e_state = (r // P == c // (NB * P)) & (r % P == c % P)
    gl_io_r, gl_st_r = (rr // H) % NB, (rr // P) % NB
    gl_io_c, gl_st_c = (c // H) % NB, (c // P) % NB
    consts = (e_out, e_state, gl_io_r == gl_io_c, gl_io_r == gl_st_c, gl_st_r == gl_io_c)
    return [jnp.asarray(a, BF16) for a in consts]


def _s5_core_kernel(h_ref, wc_ref, qc_ref, pc_ref, m_ref, d_ref, eo_ref, es_ref, mw_ref, mq_ref, mp_ref,
                    z_ref, carry_scr, b_scr, s_scr, w_scr, q_scr, p_scr, *, rows):
    half = S5_GB * S5_STATE
    strip = 2 * LANES

    @pl.when(pl.program_id(1) == 0)
    def _():
        carry_scr[...] = jnp.zeros_like(carry_scr)
        for c_ref, e_ref, k_ref, dst in ((wc_ref, eo_ref, mw_ref, w_scr), (qc_ref, es_ref, mq_ref, q_scr),
                                         (pc_ref, eo_ref, mp_ref, p_scr)):
            for c0 in range(0, S5_SUB * LANES, strip):
                cs = slice(c0, c0 + strip)
                dst[:, cs] = (_dot(c_ref[0], e_ref[:, cs]) * k_ref[:, cs]).astype(BF16)

    h_t = [h_ref[pl.ds(t, rows, stride=S5_SUB), :] for t in range(S5_SUB)]
    x = jnp.concatenate(h_t, axis=1).astype(BF16)
    b_scr[...] = _dot(x, q_scr[...])

    first_row = lax.broadcasted_iota(jnp.int32, (8, half), 0) == 0
    c_re = carry_scr[:, :half]
    c_im = carry_scr[:, half:]
    for tile in range(rows // 8):
        r0 = tile * 8
        x_re = b_scr[r0:r0 + 8, :half]
        x_im = b_scr[r0:r0 + 8, half:]
        for lvl, sh in enumerate((1, 2, 4)):
            a_re = m_ref[0, 16 * lvl:16 * lvl + 8, :]
            a_im = m_ref[0, 16 * lvl + 8:16 * lvl + 16, :]
            r_re = pltpu.roll(x_re, sh, 0)
            r_im = pltpu.roll(x_im, sh, 0)
            x_re, x_im = (x_re + a_re * r_re - a_im * r_im,
                          x_im + a_re * r_im + a_im * r_re)
        p_re = m_ref[0, 48:56, :]
        p_im = m_ref[0, 56:64, :]
        s_re = x_re + p_re * c_re - p_im * c_im
        s_im = x_im + p_re * c_im + p_im * c_re
        s_scr[r0:r0 + 8, :half] = jnp.where(first_row, c_re, pltpu.roll(s_re, 1, 0))
        s_scr[r0:r0 + 8, half:] = jnp.where(first_row, c_im, pltpu.roll(s_im, 1, 0))
        c_re = jnp.broadcast_to(s_re[7:8, :], (8, half))
        c_im = jnp.broadcast_to(s_im[7:8, :], (8, half))
    carry_scr[:, :half] = c_re
    carry_scr[:, half:] = c_im

    s = s_scr[...].astype(BF16)
    for c0 in range(0, S5_SUB * LANES, strip):
        y = _dot(x[:, :c0 + strip], w_scr[:c0 + strip, c0:c0 + strip]) + _dot(s, p_scr[:, c0:c0 + strip])
        for t in range(c0 // LANES, (c0 + strip) // LANES):
            y_t = y[:, t * LANES - c0:(t + 1) * LANES - c0] + d_ref[...] * h_t[t]
            z_ref[pl.ds(t, rows, stride=S5_SUB), :] = _gelu_tanh(y_t)


def _s5_core(h, w, q, p, mult, d_skip, *, tm):
    L = h.shape[0]
    rows = tm // S5_SUB
    wide = S5_SUB * LANES
    compact = pl.BlockSpec((1, wide, LANES), lambda g, i: (g, 0, 0))
    expand = pl.BlockSpec((LANES, wide), lambda g, i: (0, 0), pipeline_mode=pl.Buffered(1))
    mask = pl.BlockSpec((wide, wide), lambda g, i: (0, 0), pipeline_mode=pl.Buffered(1))
    table = pltpu.VMEM((wide, wide), BF16)
    return pl.pallas_call(
        functools.partial(_s5_core_kernel, rows=rows),
        grid=(S5_GB, L // tm),
        in_specs=[
            pl.BlockSpec((tm, LANES), lambda g, i: (i, g)),
            compact, compact, compact,
            pl.BlockSpec((1, 64, S5_GB * S5_STATE), lambda g, i: (g, 0, 0)),
            pl.BlockSpec((1, LANES), lambda g, i: (0, g)),
            expand, expand, mask, mask, mask,
        ],
        out_specs=pl.BlockSpec((tm, LANES), lambda g, i: (i, g)),
        out_shape=jax.ShapeDtypeStruct((L, D_MODEL), F32),
        scratch_shapes=[pltpu.VMEM((8, wide), F32), pltpu.VMEM((rows, wide), F32),
                        pltpu.VMEM((rows, wide), F32), table, table, table],
        compiler_params=_params("parallel", "arbitrary"),
        name="s5_core",
    )(h, w, q, p, mult, d_skip, *_s5_expanders())


def _s5_mixer_layer(x, g, lam_re, lam_im, log_dt, b_re, b_im, c_re, c_im, d_skip, w_glu, *, tm, tm_core):
    w, q, p, mult = _s5_tables(lam_re, lam_im, log_dt, b_re, b_im, c_re, c_im)
    h = _rms_layer(x, g, tm=tm)
    z = _s5_core(h, w, q, p, mult, d_skip.reshape(1, D_MODEL), tm=tm_core)
    return z, w_glu.astype(BF16)


def _gla_kernel(x_ref, g_ref, win_ref, wa1_ref, wa2_ref, ba_ref, ng_ref,
                wo_ref, tri_ref, o_ref, st_scr, o_scr, upd_scr, stb_scr, *, tm):
    @pl.when(pl.program_id(0) == 0)
    def _():
        st_scr[...] = jnp.zeros_like(st_scr)

    x = x_ref[...]
    h = _rms(x, g_ref[...]).astype(BF16)
    c_k, c_v, c_g = GLA_DK, 2 * GLA_DK, 2 * GLA_DK + GLA_DV
    q = (_dot(h, win_ref[:, :c_k]) * (GLA_DKH ** -0.5)).astype(BF16)
    k = _dot(h, win_ref[:, c_k:c_v])
    v = _dot(h, win_ref[:, c_v:c_g]).astype(BF16)
    gate = _dot(h, win_ref[:, c_g:c_g + GLA_DV])
    a_lo = _dot(h, wa1_ref[...]).astype(BF16)
    log_a = jax.nn.log_sigmoid(_dot(a_lo, wa2_ref[...]) + ba_ref[...]) / GLA_TEMP

    la_hi = log_a.astype(BF16)
    la_lo = (log_a - la_hi.astype(F32)).astype(BF16)
    cum = _dot(tri_ref[...], la_hi) + _dot(tri_ref[...], la_lo)
    n_chunks = tm // CHUNK
    tot_rows = [cum[(c + 1) * CHUNK - 1:(c + 1) * CHUNK, :] for c in range(n_chunks)]
    tot = jnp.concatenate([jnp.broadcast_to(t, (CHUNK, GLA_DK)) for t in tot_rows], axis=0)
    k_dec = (k * jnp.exp(tot - cum)).astype(BF16)

    heads = [(slice(hd * GLA_DKH, (hd + 1) * GLA_DKH), slice(hd * GLA_DVH, (hd + 1) * GLA_DVH))
             for hd in range(GLA_HEADS)]
    for c in range(n_chunks):
        rs = slice(c * CHUNK, (c + 1) * CHUNK)
        for hd, (ks, vs) in enumerate(heads):
            upd_scr[c, hd] = _dot_tn(v[rs, vs], k_dec[rs, ks])
    for c in range(n_chunks):
        dec_row = jnp.exp(tot_rows[c])
        for hd, (ks, vs) in enumerate(heads):
            st = st_scr[hd] * dec_row[:, ks] + upd_scr[c, hd]
            st_scr[hd] = st
            stb_scr[c, hd] = st.astype(BF16)
    for c in range(n_chunks):
        rs = slice(c * CHUNK, (c + 1) * CHUNK)
        for hd, (ks, vs) in enumerate(heads):
            o_scr[rs, vs] = _dot_nt(q[rs, ks], stb_scr[c, hd])

    outs = []
    for hd in range(GLA_HEADS):
        vs = slice(hd * GLA_DVH, (hd + 1) * GLA_DVH)
        o = o_scr[:, vs]
        outs.append(o * lax.rsqrt(jnp.mean(o * o, axis=-1, keepdims=True) + NORM_EPS) * ng_ref[:, vs])
    o = jnp.concatenate(outs, axis=1) * (gate * jax.nn.sigmoid(gate))
    o_ref[...] = x + _dot(o.astype(BF16), wo_ref[...])


def _gla_mixer_layer(x, g, w_in_main, w_a1, w_a2, b_a, norm_g, w_o, *, tm):
    L = x.shape[0]
    wa1 = jnp.pad(w_a1.astype(BF16), ((0, 0), (0, LANES - GLA_GATE_RANK)))
    wa2 = jnp.pad(w_a2.astype(BF16), ((0, LANES - GLA_GATE_RANK), (0, 0)))
    r = jnp.arange(tm)
    tri = (((r[:, None] // CHUNK) == (r[None, :] // CHUNK)) & (r[None, :] <= r[:, None])).astype(BF16)
    full = _resident
    n_chunks = tm // CHUNK

    row = pl.BlockSpec((tm, D_MODEL), lambda i: (i, 0))
    return pl.pallas_call(
        functools.partial(_gla_kernel, tm=tm),
        grid=(L // tm,),
        in_specs=[row, full((1, D_MODEL)), full((D_MODEL, 2 * GLA_DK + 2 * GLA_DV)),
                  full((D_MODEL, LANES)), full((LANES, GLA_DK)), full((1, GLA_DK)), full((1, GLA_DV)),
                  full((GLA_DV, D_MODEL)), full((tm, tm))],
        out_specs=row,
        out_shape=jax.ShapeDtypeStruct((L, D_MODEL), F32),
        scratch_shapes=[pltpu.VMEM((GLA_HEADS, GLA_DVH, GLA_DKH), F32),
                        pltpu.VMEM((tm, GLA_DV), F32),
                        pltpu.VMEM((n_chunks, GLA_HEADS, GLA_DVH, GLA_DKH), F32),
                        pltpu.VMEM((n_chunks, GLA_HEADS, GLA_DVH, GLA_DKH), BF16)],
        compiler_params=_params("arbitrary"),
        name="gla",
    )(x, g, w_in_main, wa1, wa2, b_a.reshape(1, GLA_DK), norm_g.reshape(1, GLA_DV),
      w_o.astype(BF16), tri)


def _rope_tables(positions):
    half = ROT_DIMS // 2
    inv_freq = ROPE_THETA ** (-jnp.arange(half, dtype=F32) / half)
    ang = positions.astype(F32)[:, None] * inv_freq
    cos = jnp.cos(ang)
    sin = jnp.sin(ang)
    m = jnp.arange(LANES) % DIFF_HD
    cos_l = cos[:, m % half]
    sin_l = sin[:, m % half]
    c = jnp.where(m < ROT_DIMS, cos_l, 1.0)
    s_lo = jnp.where(m < half, -sin_l, 0.0)
    s_hi = jnp.where((m >= half) & (m < ROT_DIMS), sin_l, 0.0)
    return c, s_lo, s_hi


def _qkv_kernel(x_ref, g_ref, w_ref, c_ref, slo_ref, shi_ref, qt_ref, k_ref, vt_ref):
    h = _rms(x_ref[...], g_ref[...]).astype(BF16)
    qkv = _dot(h, w_ref[...])
    c = c_ref[...]
    s_lo = slo_ref[...]
    s_hi = shi_ref[...]
    half = ROT_DIMS // 2

    def rope(t):
        return t * c + pltpu.roll(t, LANES - half, 1) * s_lo + pltpu.roll(t, half, 1) * s_hi

    for j in range(D_MODEL // LANES):
        ls = slice(j * LANES, (j + 1) * LANES)
        q = rope(qkv[:, ls]) * (DIFF_HD ** -0.5 * LOG2E)
        qt_ref[ls, :] = q.T.astype(BF16)
        k_ref[j, 0] = rope(qkv[:, D_MODEL + j * LANES:D_MODEL + (j + 1) * LANES]).astype(BF16)
        vt_ref[j, 0] = qkv[:, 2 * D_MODEL + j * LANES:2 * D_MODEL + (j + 1) * LANES].T.astype(BF16)


def _qkv_layer(x, g, w_qkv, c, s_lo, s_hi, *, tm):
    L = x.shape[0]
    n = L // tm
    row = pl.BlockSpec((tm, D_MODEL), lambda i: (i, 0))
    tab = pl.BlockSpec((tm, LANES), lambda i: (i, 0))
    return pl.pallas_call(
        _qkv_kernel,
        grid=(n,),
        in_specs=[row, pl.BlockSpec((1, D_MODEL), lambda i: (0, 0)),
                  _resident((D_MODEL, 3 * D_MODEL)), tab, tab, tab],
        out_specs=[pl.BlockSpec((D_MODEL, tm), lambda i: (0, i)),
                   pl.BlockSpec((DIFF_HEADS, 1, tm, LANES), lambda i: (0, i, 0, 0)),
                   pl.BlockSpec((DIFF_HEADS, 1, LANES, tm), lambda i: (0, i, 0, 0))],
        out_shape=[jax.ShapeDtypeStruct((D_MODEL, L), BF16),
                   jax.ShapeDtypeStruct((DIFF_HEADS, n, tm, LANES), BF16),
                   jax.ShapeDtypeStruct((DIFF_HEADS, n, LANES, tm), BF16)],
        compiler_params=_params("parallel"),
        name="diff_qkv",
    )(x, g, w_qkv, c, s_lo, s_hi)


def _attn_kernel(qt_ref, k_ref, vt_ref, lamv_ref, sg_ref, bias_ref, o_ref, m_scr, l_scr, acc_scr,
                 sa_scr, sb_scr, ma_scr, mb_scr, *, ratio, lambda_init):
    qi = pl.program_id(1)
    i = qi // ratio
    qt = qt_ref[...]
    row = lax.broadcasted_iota(jnp.int32, qt.shape, 0)
    zero = jnp.zeros_like(qt)
    q_half = (jnp.where(row < DIFF_HD, qt, zero), jnp.where(row >= DIFF_HD, qt, zero))

    m_scr[...] = jnp.full_like(m_scr, NEG_INF)
    l_scr[...] = jnp.zeros_like(l_scr)
    acc_scr[...] = jnp.zeros_like(acc_scr)

    def scores(j, slot, masked=False):
        s_ref, smax_ref = slot
        k = k_ref[0, j]
        for hf in range(2):
            s = _dot(k, q_half[hf])
            if masked:
                s = s + bias_ref[qi % ratio]
            s_ref[hf] = s
            smax_ref[hf] = jnp.max(s, axis=0, keepdims=True)

    def update(j, slot):
        s_ref, smax_ref = slot
        vt = vt_ref[0, j]
        for hf in range(2):
            m_prev = m_scr[hf]
            m_new = jnp.maximum(m_prev, smax_ref[hf])
            alpha = jnp.exp2(m_prev - m_new)
            p = jnp.exp2(s_ref[hf] - m_new)
            l_scr[hf] = alpha * l_scr[hf] + jnp.sum(p, axis=0, keepdims=True)
            acc_scr[hf] = alpha * acc_scr[hf] + _dot(vt, p.astype(BF16))
            m_scr[hf] = m_new

    slot_a, slot_b = (sa_scr, ma_scr), (sb_scr, mb_scr)

    def pair(j):
        scores(j + 1, slot_b)
        update(j, slot_a)
        scores(j + 2, slot_a)
        update(j + 1, slot_b)

    unroll = 4

    def trip(t, carry):
        for u in range(unroll):
            pair(2 * (unroll * t + u))
        return carry

    @pl.when(i == 0)
    def _():
        scores(0, slot_a, masked=True)
        update(0, slot_a)

    @pl.when(i > 0)
    def _():
        scores(0, slot_a)

    n_pairs = jnp.maximum(i - 1, 0) // 2
    lax.fori_loop(0, n_pairs // unroll, trip, 0)
    done = (n_pairs // unroll) * unroll
    for width in (2, 1):
        @pl.when((n_pairs // width) % 2 == 1)
        def _(width=width, done=done):
            for u in range(width):
                pair(2 * (done + u))
        done = done + jnp.where((n_pairs // width) % 2 == 1, width, 0)

    @pl.when(i % 2 == 1)
    def _():
        scores(i, slot_b, masked=True)
        update(i - 1, slot_a)
        update(i, slot_b)

    @pl.when((i % 2 == 0) & (i > 0))
    def _():
        scores(i - 1, slot_b)
        update(i - 2, slot_a)
        scores(i, slot_a, masked=True)
        update(i - 1, slot_b)
        update(i, slot_a)

    lv = lamv_ref[...]
    lam = (jnp.exp(jnp.sum(lv[0:1] * lv[1:2], axis=-1, keepdims=True))
           - jnp.exp(jnp.sum(lv[2:3] * lv[3:4], axis=-1, keepdims=True)) + lambda_init)
    o = acc_scr[0] / l_scr[0] - lam * (acc_scr[1] / l_scr[1])
    o = o * lax.rsqrt(jnp.mean(o * o, axis=0, keepdims=True) + SUBLN_EPS) * sg_ref[...]
    o_ref[...] = (o * (1.0 - lambda_init)).astype(BF16)


def _attn_layer(qt, k, vt, lamv, subln_g, *, tq, lambda_init):
    n, tk = k.shape[1], k.shape[2]
    L = n * tk
    ratio = tk // tq
    stat = pltpu.VMEM((2, 1, tq), F32)
    k_chunk = np.arange(tk)[:, None] // CHUNK
    q_chunk = np.arange(tq)[None, :] // CHUNK
    bias = np.stack([np.where(k_chunk <= r * (tq // CHUNK) + q_chunk, 0.0, NEG_INF) for r in range(ratio)])
    return pl.pallas_call(
        functools.partial(_attn_kernel, ratio=ratio, lambda_init=lambda_init),
        grid=(DIFF_HEADS, L // tq),
        in_specs=[
            pl.BlockSpec((LANES, tq), lambda p, i: (p, i)),
            pl.BlockSpec((1, n, tk, LANES), lambda p, i: (p, 0, 0, 0)),
            pl.BlockSpec((1, n, LANES, tk), lambda p, i: (p, 0, 0, 0)),
            pl.BlockSpec((4, DIFF_HD), lambda p, i: (0, 0)),
            pl.BlockSpec((LANES, 1), lambda p, i: (0, 0)),
            pl.BlockSpec((ratio, tk, tq), lambda p, i: (0, 0, 0), pipeline_mode=pl.Buffered(1)),
        ],
        out_specs=pl.BlockSpec((LANES, tq), lambda p, i: (p, i)),
        out_shape=jax.ShapeDtypeStruct((D_MODEL, L), BF16),
        scratch_shapes=[stat, stat, pltpu.VMEM((2, LANES, tq), F32),
                        pltpu.VMEM((2, tk, tq), F32), pltpu.VMEM((2, tk, tq), F32), stat, stat],
        compiler_params=_params("parallel", "arbitrary"),
        name="diff_attn",
    )(qt, k, vt, lamv, subln_g, jnp.asarray(bias, F32))


def _diff_mixer_layer(x, g, positions, w_qkv, lam_q1, lam_k1, lam_q2, lam_k2, subln_g, w_o,
                      lambda_init, *, tq, tk):
    c, s_lo, s_hi = _rope_tables(positions)
    qt, k4, vt4 = _qkv_layer(x, g, w_qkv.astype(BF16), c, s_lo, s_hi, tm=tk)
    lamv = jnp.stack([lam_q1, lam_k1, lam_q2, lam_k2]).astype(F32)
    at = _attn_layer(qt, k4, vt4, lamv, subln_g.reshape(LANES, 1).astype(F32), tq=tq,
                     lambda_init=lambda_init)
    return at, w_o.astype(BF16)


def _trunk(x, positions, norm_mix, norm_ffn, norm_final,
           s5_lam_re, s5_lam_im, s5_log_dt, s5_b_re, s5_b_im, s5_c_re, s5_c_im, s5_d, s5_w_glu,
           gla_w_in, gla_w_a2, gla_b_a, gla_norm, gla_w_o,
           diff_w_qkv, diff_lam_q1, diff_lam_k1, diff_lam_q2, diff_lam_k2, diff_subln, diff_w_o,
           ffn_w_gate_up, ffn_w_down, *, tm, tm_core, tm_gla, tq, tk, tm_ffn, th):
    depth = norm_mix.shape[0]
    gfinal = norm_final.reshape(1, D_MODEL)
    gla_w_a1 = gla_w_in[:, :, 2 * GLA_DK + 2 * GLA_DV:]
    gla_w_in = _to_bf16(gla_w_in, cols=2 * GLA_DK + 2 * GLA_DV)
    s5_w_glu, gla_w_o, diff_w_qkv, diff_w_o, ffn_w_gate_up, ffn_w_down = (
        _to_bf16(w) for w in (s5_w_glu, gla_w_o, diff_w_qkv, diff_w_o, ffn_w_gate_up, ffn_w_down))
    for layer in range(depth):
        kind = layer % N_MIXERS
        idx = layer // N_MIXERS
        g = norm_mix[layer].reshape(1, D_MODEL)
        if kind == 0:
            pre, tm_f = "s5", tm
            pre_args = _s5_mixer_layer(x, g, s5_lam_re[idx], s5_lam_im[idx], s5_log_dt[idx],
                                       s5_b_re[idx], s5_b_im[idx], s5_c_re[idx], s5_c_im[idx],
                                       s5_d[idx], s5_w_glu[idx], tm=tm, tm_core=tm_core)
        elif kind == 1:
            pre, tm_f, pre_args = None, tm_ffn, ()
            x = _gla_mixer_layer(x, g, gla_w_in[idx], gla_w_a1[idx], gla_w_a2[idx], gla_b_a[idx],
                                 gla_norm[idx], gla_w_o[idx], tm=tm_gla)
        else:
            pre, tm_f = "proj", tm
            lambda_init = 0.8 - 0.6 * math.exp(-0.3 * layer)
            pre_args = _diff_mixer_layer(x, g, positions, diff_w_qkv[idx], diff_lam_q1[idx],
                                         diff_lam_k1[idx], diff_lam_q2[idx], diff_lam_k2[idx],
                                         diff_subln[idx], diff_w_o[idx], lambda_init, tq=tq, tk=tk)
        x = _ffn_layer(x, pre_args, norm_ffn[layer].reshape(1, D_MODEL), ffn_w_gate_up, ffn_w_down,
                       gfinal, layer=layer, final=(layer == depth - 1), tm=tm_f, th=th, pre=pre)
    return x


def kernel(x, positions, norm_mix, norm_ffn, norm_final, s5_lam_re, s5_lam_im, s5_log_dt, s5_b_re, s5_b_im, s5_c_re, s5_c_im, s5_d, s5_w_glu, gla_w_in, gla_w_a2, gla_b_a, gla_norm, gla_w_o, diff_w_qkv, diff_lam_q1, diff_lam_k1, diff_lam_q2, diff_lam_k2, diff_subln, diff_w_o, ffn_w_gate_up, ffn_w_down):
    bsz, seq, _ = x.shape
    outs = []
    for b in range(bsz):
        outs.append(_trunk(
            x[b], positions[b], norm_mix, norm_ffn, norm_final,
            s5_lam_re, s5_lam_im, s5_log_dt, s5_b_re, s5_b_im, s5_c_re, s5_c_im, s5_d, s5_w_glu,
            gla_w_in, gla_w_a2, gla_b_a, gla_norm, gla_w_o,
            diff_w_qkv, diff_lam_q1, diff_lam_k1, diff_lam_q2, diff_lam_k2, diff_subln, diff_w_o,
            ffn_w_gate_up, ffn_w_down,
            tm=512, tm_core=4096, tm_gla=512, tq=512, tk=1024, tm_ffn=1024, th=256))
    return jnp.stack(outs)
```

```python
import functools
import math

import jax
import jax.numpy as jnp
import numpy as np
from jax import lax
from jax.experimental import pallas as pl
from jax.experimental.pallas import tpu as pltpu

F32 = jnp.float32
BF16 = jnp.bfloat16

D_MODEL = 1024
DEPTH = 4
CHUNK = 64
N_MIXERS = 3
NORM_EPS = 1e-6
S5_GROUP = 16
S5_GROUPS = D_MODEL // S5_GROUP
S5_STATE = 64
S5_SUB = 8
S5_GB = 8
GLA_HEADS = 4
GLA_DK = D_MODEL // 2
GLA_DV = D_MODEL
GLA_DKH = GLA_DK // GLA_HEADS
GLA_DVH = GLA_DV // GLA_HEADS
GLA_GATE_RANK = 16
GLA_TEMP = 16.0
DIFF_HD = 64
DIFF_HEADS = D_MODEL // (2 * DIFF_HD)
ROT_DIMS = DIFF_HD // 4
ROPE_THETA = 500000.0
NEG_INF = -1e30
SUBLN_EPS = 1e-5
LOG2E = math.log2(math.e)
FFN_HIDDEN = -(-8 * D_MODEL // (3 * 256)) * 256

LANES = 128
VMEM_LIMIT_BYTES = 56 * 1024 * 1024


def _params(*sem):
    return pltpu.CompilerParams(dimension_semantics=sem, vmem_limit_bytes=VMEM_LIMIT_BYTES)


def _rms(x, g, eps=NORM_EPS):
    return x * lax.rsqrt(jnp.mean(x * x, axis=-1, keepdims=True) + eps) * g


def _dot(a, b):
    return jnp.dot(a, b, preferred_element_type=F32)


def _dot_nt(a, b):
    return lax.dot_general(a, b, (((1,), (1,)), ((), ())), preferred_element_type=F32)


def _dot_tn(a, b):
    return lax.dot_general(a, b, (((0,), (0,)), ((), ())), preferred_element_type=F32)


def _cast_kernel(w_ref, o_ref):
    o_ref[...] = w_ref[...].astype(BF16)


def _to_bf16(w, cols=None):
    n_layers, rows, full = w.shape
    cols = full if cols is None else cols
    row_blocks = 4
    spec = pl.BlockSpec((1, rows // row_blocks, cols), lambda l, r: (l, r, 0))
    return pl.pallas_call(
        _cast_kernel,
        grid=(n_layers, row_blocks),
        in_specs=[spec],
        out_specs=spec,
        out_shape=jax.ShapeDtypeStruct((n_layers, rows, cols), BF16),
        compiler_params=_params("parallel", "parallel"),
        name="to_bf16",
    )(w)


def _gelu_tanh(x):
    return x * (0.5 * (1.0 + jnp.tanh(math.sqrt(2.0 / math.pi) * (x + 0.044715 * (x * x * x)))))


def _ffn_kernel(*refs, th, final, pre):
    if pre == "s5":
        x_ref, z_ref, wglu_ref, g_ref, wgu_ref, wd_ref, gf_ref, o_ref, h_scr = refs
        ab = _dot(z_ref[...].astype(BF16), wglu_ref[...])
        x = x_ref[...] + ab[:, :D_MODEL] * jax.nn.sigmoid(ab[:, D_MODEL:])
    elif pre == "proj":
        x_ref, at_ref, wo_ref, g_ref, wgu_ref, wd_ref, gf_ref, o_ref, h_scr = refs
        x = x_ref[...] + _dot_tn(at_ref[...], wo_ref[...])
    else:
        x_ref, g_ref, wgu_ref, wd_ref, gf_ref, o_ref, h_scr = refs
        x = x_ref[...]
    h_scr[...] = _rms(x, g_ref[...]).astype(BF16)
    o_ref[...] = x
    for c in range(FFN_HIDDEN // th):
        lo, hi = c * th, (c + 1) * th
        gate = _dot(h_scr[...], wgu_ref[0, :, lo:hi])
        up = _dot(h_scr[...], wgu_ref[0, :, FFN_HIDDEN + lo:FFN_HIDDEN + hi])
        act = (gate * jax.nn.sigmoid(gate) * up).astype(BF16)
        o_ref[...] += _dot(act, wd_ref[0, lo:hi, :])
    if final:
        o_ref[...] = _rms(o_ref[...], gf_ref[...])


def _resident(shape):
    return pl.BlockSpec(shape, lambda i: (0,) * len(shape), pipeline_mode=pl.Buffered(1))


def _ffn_layer(x, pre_args, g, w_gate_up, w_down, gf, *, layer, final, tm, th, pre=None):
    L = x.shape[0]
    row = pl.BlockSpec((tm, D_MODEL), lambda i: (i, 0))
    vec = pl.BlockSpec((1, D_MODEL), lambda i: (0, 0))

    def slab(rows, cols):
        return pl.BlockSpec((1, rows, cols), lambda i: (layer, 0, 0), pipeline_mode=pl.Buffered(1))

    if pre == "s5":
        pre_specs = [row, _resident((D_MODEL, 2 * D_MODEL))]
    elif pre == "proj":
        pre_specs = [pl.BlockSpec((D_MODEL, tm), lambda i: (0, i)), _resident((D_MODEL, D_MODEL))]
    else:
        pre_specs = []
    return pl.pallas_call(
        functools.partial(_ffn_kernel, th=th, final=final, pre=pre),
        grid=(L // tm,),
        in_specs=[row] + pre_specs + [vec, slab(D_MODEL, 2 * FFN_HIDDEN), slab(FFN_HIDDEN, D_MODEL), vec],
        out_specs=row,
        out_shape=jax.ShapeDtypeStruct((L, D_MODEL), F32),
        scratch_shapes=[pltpu.VMEM((tm, D_MODEL), BF16)],
        compiler_params=_params("parallel"),
        name="ffn" if pre is None else "ffn_" + pre,
    )(x, *pre_args, g, w_gate_up, w_down, gf)


def _s5_tables(lam_re, lam_im, log_dt, b_re, b_im, c_re, c_im):
    G, P, H, T, NB = S5_GROUPS, S5_STATE, S5_GROUP, S5_SUB, S5_GB
    f32 = F32
    lr = lam_re.astype(f32)
    li = lam_im.astype(f32)
    dt = jnp.exp(log_dt.astype(f32))[:, None]
    ab_mag = jnp.exp(lr * dt)
    ab_ang = li * dt
    ab_re = ab_mag * jnp.cos(ab_ang)
    ab_im = ab_mag * jnp.sin(ab_ang)
    den = lr * lr + li * li
    f_re = ((ab_re - 1.0) * lr + ab_im * li) / den
    f_im = (ab_im * lr - (ab_re - 1.0) * li) / den
    br = b_re.astype(f32)
    bi = b_im.astype(f32)
    bb_re = f_re[..., None] * br - f_im[..., None] * bi
    bb_im = f_re[..., None] * bi + f_im[..., None] * br
    cr = c_re.astype(f32)
    ci = c_im.astype(f32)

    def powers(n):
        n = n.astype(f32)[:, None, None]
        mag = jnp.exp(lr * dt * n)
        ang = li * dt * n
        return mag * jnp.cos(ang), mag * jnp.sin(ang)

    pr, pi = powers(jnp.arange(T + 1))
    cp_re = cr[None] * pr[:, :, None, :] - ci[None] * pi[:, :, None, :]
    cp_im = cr[None] * pi[:, :, None, :] + ci[None] * pr[:, :, None, :]
    hp = lax.Precision.HIGHEST
    kern = (jnp.einsum("ngkp,gph->ngkh", cp_re[:T], bb_re, precision=hp)
            - jnp.einsum("ngkp,gph->ngkh", cp_im[:T], bb_im, precision=hp))

    wide = T * NB * H
    t_in = jnp.arange(T)[:, None]
    t_out = jnp.arange(T)[None, :]
    lag = jnp.clip(t_out - t_in, 0, T - 1)
    causal = (t_out >= t_in).astype(f32)
    xk = kern.transpose(1, 0, 3, 2).reshape(NB, NB, T, H, H)
    xk = xk.transpose(0, 2, 1, 3, 4).reshape(NB, T, NB * H, H)
    xk = xk[:, lag] * causal[None, :, :, None, None]
    w = xk.transpose(0, 1, 3, 2, 4).reshape(NB, wide, T * H)

    prq = pr[T - 1 - jnp.arange(T)]
    piq = pi[T - 1 - jnp.arange(T)]
    qv_re = prq[..., None] * bb_re[None] - piq[..., None] * bb_im[None]
    qv_im = prq[..., None] * bb_im[None] + piq[..., None] * bb_re[None]
    qv = jnp.stack([qv_re, qv_im]).reshape(2, T, NB, NB, P, H)
    q = qv.transpose(2, 1, 3, 5, 0, 4).reshape(NB, wide, 2 * P)

    pm = jnp.stack([cp_re[1:], -cp_im[1:]]).reshape(2, T, NB, NB, H, P)
    p = pm.transpose(2, 0, 3, 5, 1, 4).reshape(NB, wide, T * H)
    w, q, p = w.astype(BF16), q.astype(BF16), p.astype(BF16)

    sr, si = powers(T * jnp.arange(1, 9))
    rows = jnp.arange(8)
    kinds = []
    for sh in (1, 2, 4):
        m = (rows >= sh).astype(f32)[:, None, None]
        kinds += [m * sr[sh - 1][None], m * si[sh - 1][None]]
    kinds += [sr, si]
    mult = jnp.stack(kinds)
    mult = mult.reshape(8, 8, NB, NB * P).transpose(2, 0, 1, 3).reshape(NB, 64, NB * P)
    return w, q, p, mult


def _rms_kernel(x_ref, g_ref, o_ref):
    o_ref[...] = _rms(x_ref[...], g_ref[...])


def _rms_layer(x, g, *, tm):
    L = x.shape[0]
    return pl.pallas_call(
        _rms_kernel,
        grid=(L // tm,),
        in_specs=[pl.BlockSpec((tm, D_MODEL), lambda i: (i, 0)),
                  pl.BlockSpec((1, D_MODEL), lambda i: (0, 0))],
        out_specs=pl.BlockSpec((tm, D_MODEL), lambda i: (i, 0)),
        out_shape=jax.ShapeDtypeStruct((L, D_MODEL), F32),
        compiler_params=_params("parallel"),
        name="s5_norm",
    )(x, g)


def _s5_expanders():
    T, NB, H, P = S5_SUB, S5_GB, S5_GROUP, S5_STATE
    wide = T * NB * H
    r = np.arange(LANES)[:, None]
    c = np.arange(wide)[None, :]
    rr = np.arange(wide)[:, None]
    e_out = (r // H == c // (NB * H)) & (r % H == c % H)
    e_state = (r // P == c // (NB * P)) & (r % P == c % P)
    gl_io_r, gl_st_r = (rr // H) % NB, (rr // P) % NB
    gl_io_c, gl_st_c = (c // H) % NB, (c // P) % NB
    consts = (e_out, e_state, gl_io_r == gl_io_c, gl_io_r == gl_st_c, gl_st_r == gl_io_c)
    return [jnp.asarray(a, BF16) for a in consts]


def _s5_core_kernel(h_ref, wc_ref, qc_ref, pc_ref, m_ref, d_ref, eo_ref, es_ref, mw_ref, mq_ref, mp_ref,
                    z_ref, carry_scr, b_scr, s_scr, w_scr, q_scr, p_scr, *, rows):
    half = S5_GB * S5_STATE
    strip = 2 * LANES

    @pl.when(pl.program_id(1) == 0)
    def _():
        carry_scr[...] = jnp.zeros_like(carry_scr)
        for c_ref, e_ref, k_ref, dst in ((wc_ref, eo_ref, mw_ref, w_scr), (qc_ref, es_ref, mq_ref, q_scr),
                                         (pc_ref, eo_ref, mp_ref, p_scr)):
            for c0 in range(0, S5_SUB * LANES, strip):
                cs = slice(c0, c0 + strip)
                dst[:, cs] = (_dot(c_ref[0], e_ref[:, cs]) * k_ref[:, cs]).astype(BF16)

    h_t = [h_ref[pl.ds(t, rows, stride=S5_SUB), :] for t in range(S5_SUB)]
    x = jnp.concatenate(h_t, axis=1).astype(BF16)
    b_scr[...] = _dot(x, q_scr[...])

    first_row = lax.broadcasted_iota(jnp.int32, (8, half), 0) == 0
    c_re = carry_scr[:, :half]
    c_im = carry_scr[:, half:]
    for tile in range(rows // 8):
        r0 = tile * 8
        x_re = b_scr[r0:r0 + 8, :half]
        x_im = b_scr[r0:r0 + 8, half:]
        for lvl, sh in enumerate((1, 2, 4)):
            a_re = m_ref[0, 16 * lvl:16 * lvl + 8, :]
            a_im = m_ref[0, 16 * lvl + 8:16 * lvl + 16, :]
            r_re = pltpu.roll(x_re, sh, 0)
            r_im = pltpu.roll(x_im, sh, 0)
            x_re, x_im = (x_re + a_re * r_re - a_im * r_im,
                          x_im + a_re * r_im + a_im * r_re)
        p_re = m_ref[0, 48:56, :]
        p_im = m_ref[0, 56:64, :]
        s_re = x_re + p_re * c_re - p_im * c_im
        s_im = x_im + p_re * c_im + p_im * c_re
        s_scr[r0:r0 + 8, :half] = jnp.where(first_row, c_re, pltpu.roll(s_re, 1, 0))
        s_scr[r0:r0 + 8, half:] = jnp.where(first_row, c_im, pltpu.roll(s_im, 1, 0))
        c_re = jnp.broadcast_to(s_re[7:8, :], (8, half))
        c_im = jnp.broadcast_to(s_im[7:8, :], (8, half))
    carry_scr[:, :half] = c_re
    carry_scr[:, half:] = c_im

    s = s_scr[...].astype(BF16)
    for c0 in range(0, S5_SUB * LANES, strip):
        y = _dot(x[:, :c0 + strip], w_scr[:c0 + strip, c0:c0 + strip]) + _dot(s, p_scr[:, c0:c0 + strip])
        for t in range(c0 // LANES, (c0 + strip) // LANES):
            y_t = y[:, t * LANES - c0:(t + 1) * LANES - c0] + d_ref[...] * h_t[t]
            z_ref[pl.ds(t, rows, stride=S5_SUB), :] = _gelu_tanh(y_t)


def _s5_core(h, w, q, p, mult, d_skip, *, tm):
    L = h.shape[0]
    rows = tm // S5_SUB
    wide = S5_SUB * LANES
    compact = pl.BlockSpec((1, wide, LANES), lambda g, i: (g, 0, 0))
    expand = pl.BlockSpec((LANES, wide), lambda g, i: (0, 0), pipeline_mode=pl.Buffered(1))
    mask = pl.BlockSpec((wide, wide), lambda g, i: (0, 0), pipeline_mode=pl.Buffered(1))
    table = pltpu.VMEM((wide, wide), BF16)
    return pl.pallas_call(
        functools.partial(_s5_core_kernel, rows=rows),
        grid=(S5_GB, L // tm),
        in_specs=[
            pl.BlockSpec((tm, LANES), lambda g, i: (i, g)),
            compact, compact, compact,
            pl.BlockSpec((1, 64, S5_GB * S5_STATE), lambda g, i: (g, 0, 0)),
            pl.BlockSpec((1, LANES), lambda g, i: (0, g)),
            expand, expand, mask, mask, mask,
        ],
        out_specs=pl.BlockSpec((tm, LANES), lambda g, i: (i, g)),
        out_shape=jax.ShapeDtypeStruct((L, D_MODEL), F32),
        scratch_shapes=[pltpu.VMEM((8, wide), F32), pltpu.VMEM((rows, wide), F32),
                        pltpu.VMEM((rows, wide), F32), table, table, table],
        compiler_params=_params("parallel", "arbitrary"),
        name="s5_core",
    )(h, w, q, p, mult, d_skip, *_s5_expanders())


def _s5_mixer_layer(x, g, lam_re, lam_im, log_dt, b_re, b_im, c_re, c_im, d_skip, w_glu, *, tm, tm_core):
    w, q, p, mult = _s5_tables(lam_re, lam_im, log_dt, b_re, b_im, c_re, c_im)
    h = _rms_layer(x, g, tm=tm)
    z = _s5_core(h, w, q, p, mult, d_skip.reshape(1, D_MODEL), tm=tm_core)
    return z, w_glu.astype(BF16)


def _gla_kernel(x_ref, g_ref, win_ref, wa1_ref, wa2_ref, ba_ref, ng_ref,
                wo_ref, tri_ref, o_ref, st_scr, o_scr, upd_scr, stb_scr, *, tm):
    @pl.when(pl.program_id(0) == 0)
    def _():
        st_scr[...] = jnp.zeros_like(st_scr)

    x = x_ref[...]
    h = _rms(x, g_ref[...]).astype(BF16)
    c_k, c_v, c_g = GLA_DK, 2 * GLA_DK, 2 * GLA_DK + GLA_DV
    q = (_dot(h, win_ref[:, :c_k]) * (GLA_DKH ** -0.5)).astype(BF16)
    k = _dot(h, win_ref[:, c_k:c_v])
    v = _dot(h, win_ref[:, c_v:c_g]).astype(BF16)
    gate = _dot(h, win_ref[:, c_g:c_g + GLA_DV])
    a_lo = _dot(h, wa1_ref[...]).astype(BF16)
    log_a = jax.nn.log_sigmoid(_dot(a_lo, wa2_ref[...]) + ba_ref[...]) / GLA_TEMP

    la_hi = log_a.astype(BF16)
    la_lo = (log_a - la_hi.astype(F32)).astype(BF16)
    cum = _dot(tri_ref[...], la_hi) + _dot(tri_ref[...], la_lo)
    n_chunks = tm // CHUNK
    tot_rows = [cum[(c + 1) * CHUNK - 1:(c + 1) * CHUNK, :] for c in range(n_chunks)]
    tot = jnp.concatenate([jnp.broadcast_to(t, (CHUNK, GLA_DK)) for t in tot_rows], axis=0)
    k_dec = (k * jnp.exp(tot - cum)).astype(BF16)

    heads = [(slice(hd * GLA_DKH, (hd + 1) * GLA_DKH), slice(hd * GLA_DVH, (hd + 1) * GLA_DVH))
             for hd in range(GLA_HEADS)]
    for c in range(n_chunks):
        rs = slice(c * CHUNK, (c + 1) * CHUNK)
        for hd, (ks, vs) in enumerate(heads):
            upd_scr[c, hd] = _dot_tn(v[rs, vs], k_dec[rs, ks])
    for c in range(n_chunks):
        dec_row = jnp.exp(tot_rows[c])
        for hd, (ks, vs) in enumerate(heads):
            st = st_scr[hd] * dec_row[:, ks] + upd_scr[c, hd]
            st_scr[hd] = st
            stb_scr[c, hd] = st.astype(BF16)
    for c in range(n_chunks):
        rs = slice(c * CHUNK, (c + 1) * CHUNK)
        for hd, (ks, vs) in enumerate(heads):
            o_scr[rs, vs] = _dot_nt(q[rs, ks], stb_scr[c, hd])

    outs = []
    for hd in range(GLA_HEADS):
        vs = slice(hd * GLA_DVH, (hd + 1) * GLA_DVH)
        o = o_scr[:, vs]
        outs.append(o * lax.rsqrt(jnp.mean(o * o, axis=-1, keepdims=True) + NORM_EPS) * ng_ref[:, vs])
    o = jnp.concatenate(outs, axis=1) * (gate * jax.nn.sigmoid(gate))
    o_ref[...] = x + _dot(o.astype(BF16), wo_ref[...])


def _gla_mixer_layer(x, g, w_in_main, w_a1, w_a2, b_a, norm_g, w_o, *, tm):
    L = x.shape[0]
    wa1 = jnp.pad(w_a1.astype(BF16), ((0, 0), (0, LANES - GLA_GATE_RANK)))
    wa2 = jnp.pad(w_a2.astype(BF16), ((0, LANES - GLA_GATE_RANK), (0, 0)))
    r = jnp.arange(tm)
    tri = (((r[:, None] // CHUNK) == (r[None, :] // CHUNK)) & (r[None, :] <= r[:, None])).astype(BF16)
    full = _resident
    n_chunks = tm // CHUNK

    row = pl.BlockSpec((tm, D_MODEL), lambda i: (i, 0))
    return pl.pallas_call(
        functools.partial(_gla_kernel, tm=tm),
        grid=(L // tm,),
        in_specs=[row, full((1, D_MODEL)), full((D_MODEL, 2 * GLA_DK + 2 * GLA_DV)),
                  full((D_MODEL, LANES)), full((LANES, GLA_DK)), full((1, GLA_DK)), full((1, GLA_DV)),
                  full((GLA_DV, D_MODEL)), full((tm, tm))],
        out_specs=row,
        out_shape=jax.ShapeDtypeStruct((L, D_MODEL), F32),
        scratch_shapes=[pltpu.VMEM((GLA_HEADS, GLA_DVH, GLA_DKH), F32),
                        pltpu.VMEM((tm, GLA_DV), F32),
                        pltpu.VMEM((n_chunks, GLA_HEADS, GLA_DVH, GLA_DKH), F32),
                        pltpu.VMEM((n_chunks, GLA_HEADS, GLA_DVH, GLA_DKH), BF16)],
        compiler_params=_params("arbitrary"),
        name="gla",
    )(x, g, w_in_main, wa1, wa2, b_a.reshape(1, GLA_DK), norm_g.reshape(1, GLA_DV),
      w_o.astype(BF16), tri)


def _rope_tables(positions):
    half = ROT_DIMS // 2
    inv_freq = ROPE_THETA ** (-jnp.arange(half, dtype=F32) / half)
    ang = positions.astype(F32)[:, None] * inv_freq
    cos = jnp.cos(ang)
    sin = jnp.sin(ang)
    m = np.arange(LANES) % DIFF_HD
    pick = (np.arange(half)[:, None] == (m % half)[None, :])
    e_c = jnp.asarray(pick & (m < ROT_DIMS)[None, :], F32)
    e_lo = jnp.asarray(pick & (m < half)[None, :], F32)
    e_hi = jnp.asarray(pick & ((m >= half) & (m < ROT_DIMS))[None, :], F32)
    hp = lax.Precision.HIGHEST
    c = jnp.dot(cos, e_c, precision=hp) + jnp.asarray(m >= ROT_DIMS, F32)
    s_lo = -jnp.dot(sin, e_lo, precision=hp)
    s_hi = jnp.dot(sin, e_hi, precision=hp)
    return c, s_lo, s_hi


def _qkv_kernel(x_ref, g_ref, w_ref, c_ref, slo_ref, shi_ref, qt_ref, k_ref, vt_ref):
    h = _rms(x_ref[...], g_ref[...]).astype(BF16)
    qkv = _dot(h, w_ref[...])
    c = c_ref[...]
    s_lo = slo_ref[...]
    s_hi = shi_ref[...]
    half = ROT_DIMS // 2

    def rope(t):
        return t * c + pltpu.roll(t, LANES - half, 1) * s_lo + pltpu.roll(t, half, 1) * s_hi

    for j in range(D_MODEL // LANES):
        ls = slice(j * LANES, (j + 1) * LANES)
        q = rope(qkv[:, ls]) * (DIFF_HD ** -0.5 * LOG2E)
        qt_ref[ls, :] = q.T.astype(BF16)
        k_ref[j, 0] = rope(qkv[:, D_MODEL + j * LANES:D_MODEL + (j + 1) * LANES]).astype(BF16)
        vt_ref[j, 0] = qkv[:, 2 * D_MODEL + j * LANES:2 * D_MODEL + (j + 1) * LANES].T.astype(BF16)


def _qkv_layer(x, g, w_qkv, c, s_lo, s_hi, *, tm):
    L = x.shape[0]
    n = L // tm
    row = pl.BlockSpec((tm, D_MODEL), lambda i: (i, 0))
    tab = pl.BlockSpec((tm, LANES), lambda i: (i, 0))
    return pl.pallas_call(
        _qkv_kernel,
        grid=(n,),
        in_specs=[row, pl.BlockSpec((1, D_MODEL), lambda i: (0, 0)),
                  _resident((D_MODEL, 3 * D_MODEL)), tab, tab, tab],
        out_specs=[pl.BlockSpec((D_MODEL, tm), lambda i: (0, i)),
                   pl.BlockSpec((DIFF_HEADS, 1, tm, LANES), lambda i: (0, i, 0, 0)),
                   pl.BlockSpec((DIFF_HEADS, 1, LANES, tm), lambda i: (0, i, 0, 0))],
        out_shape=[jax.ShapeDtypeStruct((D_MODEL, L), BF16),
                   jax.ShapeDtypeStruct((DIFF_HEADS, n, tm, LANES), BF16),
                   jax.ShapeDtypeStruct((DIFF_HEADS, n, LANES, tm), BF16)],
        compiler_params=_params("parallel"),
        name="diff_qkv",
    )(x, g, w_qkv, c, s_lo, s_hi)


def _attn_kernel(qt_ref, k_ref, vt_ref, lamv_ref, sg_ref, bias_ref, o_ref, m_scr, l_scr, acc_scr,
                 sa_scr, sb_scr, ma_scr, mb_scr, *, lambda_init):
    qi = pl.program_id(1)
    i = qi // 2
    qt = qt_ref[...]
    row = lax.broadcasted_iota(jnp.int32, qt.shape, 0)
    zero = jnp.zeros_like(qt)
    q_half = (jnp.where(row < DIFF_HD, qt, zero), jnp.where(row >= DIFF_HD, qt, zero))

    m_scr[...] = jnp.full_like(m_scr, NEG_INF)
    l_scr[...] = jnp.zeros_like(l_scr)
    acc_scr[...] = jnp.zeros_like(acc_scr)

    tq = qt.shape[1]

    def scores(j, slot, part=None, masked=False):
        s_ref, smax_ref = slot
        rows = slice(None) if part is None else slice(part * tq, (part + 1) * tq)
        k = k_ref[0, j, rows, :]
        for hf in range(2):
            s = _dot(k, q_half[hf])
            if masked:
                s = s + bias_ref[...]
            if part is None:
                s_ref[hf] = s
            else:
                s_ref[hf, :tq, :] = s
            smax_ref[hf] = jnp.max(s, axis=0, keepdims=True)

    def update(j, slot, part=None):
        s_ref, smax_ref = slot
        cols = slice(None) if part is None else slice(part * tq, (part + 1) * tq)
        vt = vt_ref[0, j, :, cols]
        for hf in range(2):
            m_prev = m_scr[hf]
            m_new = jnp.maximum(m_prev, smax_ref[hf])
            alpha = jnp.exp2(m_prev - m_new)
            s = s_ref[hf] if part is None else s_ref[hf, :tq, :]
            p = jnp.exp2(s - m_new)
            l_scr[hf] = alpha * l_scr[hf] + jnp.sum(p, axis=0, keepdims=True)
            acc_scr[hf] = alpha * acc_scr[hf] + _dot(vt, p.astype(BF16))
            m_scr[hf] = m_new

    slot_a, slot_b = (sa_scr, ma_scr), (sb_scr, mb_scr)

    def pair(j):
        scores(j + 1, slot_b)
        update(j, slot_a)
        scores(j + 2, slot_a)
        update(j + 1, slot_b)

    unroll = 4

    def trip(t, carry):
        for u in range(unroll):
            pair(2 * (unroll * t + u))
        return carry

    def diagonal(cond, prefix, pending, free, jp):
        @pl.when(cond & (qi % 2 == 0))
        def _():
            prefix()
            scores(i, free, part=0, masked=True)
            if jp is not None:
                update(jp, pending)
            update(i, free, part=0)

        @pl.when(cond & (qi % 2 == 1))
        def _():
            prefix()
            scores(i, free, part=0)
            if jp is not None:
                update(jp, pending)
            scores(i, pending, part=1, masked=True)
            update(i, free, part=0)
            update(i, pending, part=1)

    diagonal(i == 0, lambda: None, slot_b, slot_a, None)

    @pl.when(i > 0)
    def _():
        scores(0, slot_a)

    n_pairs = jnp.maximum(i - 1, 0) // 2
    lax.fori_loop(0, n_pairs // unroll, trip, 0)
    done = (n_pairs // unroll) * unroll
    for width in (2, 1):
        @pl.when((n_pairs // width) % 2 == 1)
        def _(width=width, done=done):
            for u in range(width):
                pair(2 * (done + u))
        done = done + jnp.where((n_pairs // width) % 2 == 1, width, 0)

    diagonal(i % 2 == 1, lambda: None, slot_a, slot_b, i - 1)

    def two_left():
        scores(i - 1, slot_b)
        update(i - 2, slot_a)

    diagonal((i % 2 == 0) & (i > 0), two_left, slot_b, slot_a, i - 1)

    lv = lamv_ref[...]
    lam = (jnp.exp(jnp.sum(lv[0:1] * lv[1:2], axis=-1, keepdims=True))
           - jnp.exp(jnp.sum(lv[2:3] * lv[3:4], axis=-1, keepdims=True)) + lambda_init)
    o = acc_scr[0] / l_scr[0] - lam * (acc_scr[1] / l_scr[1])
    o = o * lax.rsqrt(jnp.mean(o * o, axis=0, keepdims=True) + SUBLN_EPS) * sg_ref[...]
    o_ref[...] = (o * (1.0 - lambda_init)).astype(BF16)


def _attn_layer(qt, k, vt, lamv, subln_g, *, tq, lambda_init):
    n, tk = k.shape[1], k.shape[2]
    L = n * tk
    assert tk == 2 * tq, (tk, tq)
    stat = pltpu.VMEM((2, 1, tq), F32)
    chunk = np.arange(tq) // CHUNK
    bias = np.where(chunk[:, None] <= chunk[None, :], 0.0, NEG_INF)
    return pl.pallas_call(
        functools.partial(_attn_kernel, lambda_init=lambda_init),
        grid=(DIFF_HEADS, L // tq),
        in_specs=[
            pl.BlockSpec((LANES, tq), lambda p, i: (p, i)),
            pl.BlockSpec((1, n, tk, LANES), lambda p, i: (p, 0, 0, 0)),
            pl.BlockSpec((1, n, LANES, tk), lambda p, i: (p, 0, 0, 0)),
            pl.BlockSpec((4, DIFF_HD), lambda p, i: (0, 0)),
            pl.BlockSpec((LANES, 1), lambda p, i: (0, 0)),
            pl.BlockSpec((tq, tq), lambda p, i: (0, 0), pipeline_mode=pl.Buffered(1)),
        ],
        out_specs=pl.BlockSpec((LANES, tq), lambda p, i: (p, i)),
        out_shape=jax.ShapeDtypeStruct((D_MODEL, L), BF16),
        scratch_shapes=[stat, stat, pltpu.VMEM((2, LANES, tq), F32),
                        pltpu.VMEM((2, tk, tq), F32), pltpu.VMEM((2, tk, tq), F32), stat, stat],
        compiler_params=_params("parallel", "arbitrary"),
        name="diff_attn",
    )(qt, k, vt, lamv, subln_g, jnp.asarray(bias, F32))


def _diff_mixer_layer(x, g, positions, w_qkv, lam_q1, lam_k1, lam_q2, lam_k2, subln_g, w_o,
                      lambda_init, *, tq, tk):
    c, s_lo, s_hi = _rope_tables(positions)
    qt, k4, vt4 = _qkv_layer(x, g, w_qkv.astype(BF16), c, s_lo, s_hi, tm=tk)
    lamv = jnp.stack([lam_q1, lam_k1, lam_q2, lam_k2]).astype(F32)
    at = _attn_layer(qt, k4, vt4, lamv, subln_g.reshape(LANES, 1).astype(F32), tq=tq,
                     lambda_init=lambda_init)
    return at, w_o.astype(BF16)


def _trunk(x, positions, norm_mix, norm_ffn, norm_final,
           s5_lam_re, s5_lam_im, s5_log_dt, s5_b_re, s5_b_im, s5_c_re, s5_c_im, s5_d, s5_w_glu,
           gla_w_in, gla_w_a2, gla_b_a, gla_norm, gla_w_o,
           diff_w_qkv, diff_lam_q1, diff_lam_k1, diff_lam_q2, diff_lam_k2, diff_subln, diff_w_o,
           ffn_w_gate_up, ffn_w_down, *, tm, tm_core, tm_gla, tq, tk, tm_ffn, th):
    depth = norm_mix.shape[0]
    gfinal = norm_final.reshape(1, D_MODEL)
    gla_w_a1 = gla_w_in[:, :, 2 * GLA_DK + 2 * GLA_DV:]
    gla_w_in = _to_bf16(gla_w_in, cols=2 * GLA_DK + 2 * GLA_DV)
    s5_w_glu, gla_w_o, diff_w_qkv, diff_w_o, ffn_w_gate_up, ffn_w_down = (
        _to_bf16(w) for w in (s5_w_glu, gla_w_o, diff_w_qkv, diff_w_o, ffn_w_gate_up, ffn_w_down))
    for layer in range(depth):
        kind = layer % N_MIXERS
        idx = layer // N_MIXERS
        g = norm_mix[layer].reshape(1, D_MODEL)
        if kind == 0:
            pre, tm_f = "s5", tm
            pre_args = _s5_mixer_layer(x, g, s5_lam_re[idx], s5_lam_im[idx], s5_log_dt[idx],
                                       s5_b_re[idx], s5_b_im[idx], s5_c_re[idx], s5_c_im[idx],
                                       s5_d[idx], s5_w_glu[idx], tm=tm, tm_core=tm_core)
        elif kind == 1:
            pre, tm_f, pre_args = None, tm_ffn, ()
            x = _gla_mixer_layer(x, g, gla_w_in[idx], gla_w_a1[idx], gla_w_a2[idx], gla_b_a[idx],
                                 gla_norm[idx], gla_w_o[idx], tm=tm_gla)
        else:
            pre, tm_f = "proj", tm
            lambda_init = 0.8 - 0.6 * math.exp(-0.3 * layer)
            pre_args = _diff_mixer_layer(x, g, positions, diff_w_qkv[idx], diff_lam_q1[idx],
                                         diff_lam_k1[idx], diff_lam_q2[idx], diff_lam_k2[idx],
                                         diff_subln[idx], diff_w_o[idx], lambda_init, tq=tq, tk=tk)
        x = _ffn_layer(x, pre_args, norm_ffn[layer].reshape(1, D_MODEL), ffn_w_gate_up, ffn_w_down,
                       gfinal, layer=layer, final=(layer == depth - 1), tm=tm_f, th=th, pre=pre)
    return x


def kernel(x, positions, norm_mix, norm_ffn, norm_final, s5_lam_re, s5_lam_im, s5_log_dt, s5_b_re, s5_b_im, s5_c_re, s5_c_im, s5_d, s5_w_glu, gla_w_in, gla_w_a2, gla_b_a, gla_norm, gla_w_o, diff_w_qkv, diff_lam_q1, diff_lam_k1, diff_lam_q2, diff_lam_k2, diff_subln, diff_w_o, ffn_w_gate_up, ffn_w_down):
    bsz, seq, _ = x.shape
    outs = []
    for b in range(bsz):
        outs.append(_trunk(
            x[b], positions[b], norm_mix, norm_ffn, norm_final,
            s5_lam_re, s5_lam_im, s5_log_dt, s5_b_re, s5_b_im, s5_c_re, s5_c_im, s5_d, s5_w_glu,
            gla_w_in, gla_w_a2, gla_b_a, gla_norm, gla_w_o,
            diff_w_qkv, diff_lam_q1, diff_lam_k1, diff_lam_q2, diff_lam_k2, diff_subln, diff_w_o,
            ffn_w_gate_up, ffn_w_down,
            tm=512, tm_core=4096, tm_gla=512, tq=512, tk=1024, tm_ffn=1024, th=256))
    return jnp.stack(outs)
```

```python
import functools
import math

import jax
import jax.numpy as jnp
import numpy as np
from jax import lax
from jax.experimental import pallas as pl
from jax.experimental.pallas import tpu as pltpu

F32 = jnp.float32
BF16 = jnp.bfloat16

D_MODEL = 1024
DEPTH = 4
CHUNK = 64
N_MIXERS = 3
NORM_EPS = 1e-6
S5_GROUP = 16
S5_GROUPS = D_MODEL // S5_GROUP
S5_STATE = 64
S5_SUB = 8
S5_GB = 8
GLA_HEADS = 4
GLA_DK = D_MODEL // 2
GLA_DV = D_MODEL
GLA_DKH = GLA_DK // GLA_HEADS
GLA_DVH = GLA_DV // GLA_HEADS
GLA_GATE_RANK = 16
GLA_TEMP = 16.0
DIFF_HD = 64
DIFF_HEADS = D_MODEL // (2 * DIFF_HD)
ROT_DIMS = DIFF_HD // 4
ROPE_THETA = 500000.0
NEG_INF = -1e30
SUBLN_EPS = 1e-5
LOG2E = math.log2(math.e)
FFN_HIDDEN = -(-8 * D_MODEL // (3 * 256)) * 256

LANES = 128
VMEM_LIMIT_BYTES = 56 * 1024 * 1024


def _params(*sem):
    return pltpu.CompilerParams(dimension_semantics=sem, vmem_limit_bytes=VMEM_LIMIT_BYTES)


def _rms(x, g, eps=NORM_EPS):
    return x * lax.rsqrt(jnp.mean(x * x, axis=-1, keepdims=True) + eps) * g


def _dot(a, b):
    return jnp.dot(a, b, preferred_element_type=F32)


def _dot_nt(a, b):
    return lax.dot_general(a, b, (((1,), (1,)), ((), ())), preferred_element_type=F32)


def _dot_tn(a, b):
    return lax.dot_general(a, b, (((0,), (0,)), ((), ())), preferred_element_type=F32)


def _cast_kernel(w_ref, o_ref):
    o_ref[...] = w_ref[...].astype(BF16)


def _to_bf16(w, cols=None):
    n_layers, rows, full = w.shape
    cols = full if cols is None else cols
    row_blocks = 4
    spec = pl.BlockSpec((1, rows // row_blocks, cols), lambda l, r: (l, r, 0))
    return pl.pallas_call(
        _cast_kernel,
        grid=(n_layers, row_blocks),
        in_specs=[spec],
        out_specs=spec,
        out_shape=jax.ShapeDtypeStruct((n_layers, rows, cols), BF16),
        compiler_params=_params("parallel", "parallel"),
        name="to_bf16",
    )(w)


def _gelu_tanh(x):
    return x * (0.5 * (1.0 + jnp.tanh(math.sqrt(2.0 / math.pi) * (x + 0.044715 * (x * x * x)))))


def _ffn_kernel(*refs, th, final, pre):
    if pre == "s5":
        x_ref, z_ref, wglu_ref, g_ref, wgu_ref, wd_ref, gf_ref, o_ref, h_scr = refs
        ab = _dot(z_ref[...].astype(BF16), wglu_ref[...])
        x = x_ref[...] + ab[:, :D_MODEL] * jax.nn.sigmoid(ab[:, D_MODEL:])
    elif pre == "proj":
        x_ref, at_ref, wo_ref, g_ref, wgu_ref, wd_ref, gf_ref, o_ref, h_scr = refs
        x = x_ref[...] + _dot_tn(at_ref[...], wo_ref[...])
    else:
        x_ref, g_ref, wgu_ref, wd_ref, gf_ref, o_ref, h_scr = refs
        x = x_ref[...]
    h_scr[...] = _rms(x, g_ref[...]).astype(BF16)
    o_ref[...] = x
    for c in range(FFN_HIDDEN // th):
        lo, hi = c * th, (c + 1) * th
        gate = _dot(h_scr[...], wgu_ref[0, :, lo:hi])
        up = _dot(h_scr[...], wgu_ref[0, :, FFN_HIDDEN + lo:FFN_HIDDEN + hi])
        act = (gate * jax.nn.sigmoid(gate) * up).astype(BF16)
        o_ref[...] += _dot(act, wd_ref[0, lo:hi, :])
    if final:
        o_ref[...] = _rms(o_ref[...], gf_ref[...])


def _resident(shape):
    return pl.BlockSpec(shape, lambda i: (0,) * len(shape), pipeline_mode=pl.Buffered(1))


def _ffn_layer(x, pre_args, g, w_gate_up, w_down, gf, *, layer, final, tm, th, pre=None):
    L = x.shape[0]
    row = pl.BlockSpec((tm, D_MODEL), lambda i: (i, 0))
    vec = pl.BlockSpec((1, D_MODEL), lambda i: (0, 0))

    def slab(rows, cols):
        return pl.BlockSpec((1, rows, cols), lambda i: (layer, 0, 0), pipeline_mode=pl.Buffered(1))

    if pre == "s5":
        pre_specs = [row, _resident((D_MODEL, 2 * D_MODEL))]
    elif pre == "proj":
        pre_specs = [pl.BlockSpec((D_MODEL, tm), lambda i: (0, i)), _resident((D_MODEL, D_MODEL))]
    else:
        pre_specs = []
    return pl.pallas_call(
        functools.partial(_ffn_kernel, th=th, final=final, pre=pre),
        grid=(L // tm,),
        in_specs=[row] + pre_specs + [vec, slab(D_MODEL, 2 * FFN_HIDDEN), slab(FFN_HIDDEN, D_MODEL), vec],
        out_specs=row,
        out_shape=jax.ShapeDtypeStruct((L, D_MODEL), F32),
        scratch_shapes=[pltpu.VMEM((tm, D_MODEL), BF16)],
        compiler_params=_params("parallel"),
        name="ffn" if pre is None else "ffn_" + pre,
    )(x, *pre_args, g, w_gate_up, w_down, gf)


def _s5_tables(lam_re, lam_im, log_dt, b_re, b_im, c_re, c_im):
    G, P, H, T, NB = S5_GROUPS, S5_STATE, S5_GROUP, S5_SUB, S5_GB
    f32 = F32
    lr = lam_re.astype(f32)
    li = lam_im.astype(f32)
    dt = jnp.exp(log_dt.astype(f32))[:, None]
    ab_mag = jnp.exp(lr * dt)
    ab_ang = li * dt
    ab_re = ab_mag * jnp.cos(ab_ang)
    ab_im = ab_mag * jnp.sin(ab_ang)
    den = lr * lr + li * li
    f_re = ((ab_re - 1.0) * lr + ab_im * li) / den
    f_im = (ab_im * lr - (ab_re - 1.0) * li) / den
    br = b_re.astype(f32)
    bi = b_im.astype(f32)
    bb_re = f_re[..., None] * br - f_im[..., None] * bi
    bb_im = f_re[..., None] * bi + f_im[..., None] * br
    cr = c_re.astype(f32)
    ci = c_im.astype(f32)

    def powers(n):
        n = n.astype(f32)[:, None, None]
        mag = jnp.exp(lr * dt * n)
        ang = li * dt * n
        return mag * jnp.cos(ang), mag * jnp.sin(ang)

    pr, pi = powers(jnp.arange(T + 1))
    cp_re = cr[None] * pr[:, :, None, :] - ci[None] * pi[:, :, None, :]
    cp_im = cr[None] * pi[:, :, None, :] + ci[None] * pr[:, :, None, :]
    cpt_re = cp_re[:T].transpose(0, 2, 3, 1)[:, :, :, None, :]
    cpt_im = cp_im[:T].transpose(0, 2, 3, 1)[:, :, :, None, :]
    bbt_re = bb_re.transpose(1, 2, 0)[None, None]
    bbt_im = bb_im.transpose(1, 2, 0)[None, None]
    kern = jnp.sum(cpt_re * bbt_re - cpt_im * bbt_im, axis=2)
    kern = kern.transpose(0, 3, 1, 2)

    wide = T * NB * H
    t_in = jnp.arange(T)[:, None]
    t_out = jnp.arange(T)[None, :]
    lag = jnp.clip(t_out - t_in, 0, T - 1)
    causal = (t_out >= t_in).astype(f32)
    xk = kern.transpose(1, 0, 3, 2).reshape(NB, NB, T, H, H)
    xk = xk.transpose(0, 2, 1, 3, 4).reshape(NB, T, NB * H, H)
    xk = xk[:, lag] * causal[None, :, :, None, None]
    w = xk.transpose(0, 1, 3, 2, 4).reshape(NB, wide, T * H)

    prq = pr[T - 1 - jnp.arange(T)]
    piq = pi[T - 1 - jnp.arange(T)]
    qv_re = prq[..., None] * bb_re[None] - piq[..., None] * bb_im[None]
    qv_im = prq[..., None] * bb_im[None] + piq[..., None] * bb_re[None]
    qv = jnp.stack([qv_re, qv_im]).reshape(2, T, NB, NB, P, H)
    q = qv.transpose(2, 1, 3, 5, 0, 4).reshape(NB, wide, 2 * P)

    pm = jnp.stack([cp_re[1:], -cp_im[1:]]).reshape(2, T, NB, NB, H, P)
    p = pm.transpose(2, 0, 3, 5, 1, 4).reshape(NB, wide, T * H)
    w, q, p = w.astype(BF16), q.astype(BF16), p.astype(BF16)

    sr, si = powers(T * jnp.arange(1, 9))
    rows = jnp.arange(8)
    kinds = []
    for sh in (1, 2, 4):
        m = (rows >= sh).astype(f32)[:, None, None]
        kinds += [m * sr[sh - 1][None], m * si[sh - 1][None]]
    kinds += [sr, si]
    mult = jnp.stack(kinds)
    mult = mult.reshape(8, 8, NB, NB * P).transpose(2, 0, 1, 3).reshape(NB, 64, NB * P)
    return w, q, p, mult


def _rms_kernel(x_ref, g_ref, o_ref):
    o_ref[...] = _rms(x_ref[...], g_ref[...])


def _rms_layer(x, g, *, tm):
    L = x.shape[0]
    return pl.pallas_call(
        _rms_kernel,
        grid=(L // tm,),
        in_specs=[pl.BlockSpec((tm, D_MODEL), lambda i: (i, 0)),
                  pl.BlockSpec((1, D_MODEL), lambda i: (0, 0))],
        out_specs=pl.BlockSpec((tm, D_MODEL), lambda i: (i, 0)),
        out_shape=jax.ShapeDtypeStruct((L, D_MODEL), F32),
        compiler_params=_params("parallel"),
        name="s5_norm",
    )(x, g)


def _s5_expanders():
    T, NB, H, P = S5_SUB, S5_GB, S5_GROUP, S5_STATE
    wide = T * NB * H
    r = np.arange(LANES)[:, None]
    c = np.arange(wide)[None, :]
    rr = np.arange(wide)[:, None]
    e_out = (r // H == c // (NB * H)) & (r % H == c % H)
    e_state = (r // P == c // (NB * P)) & (r % P == c % P)
    gl_io_r, gl_st_r = (rr // H) % NB, (rr // P) % NB
    gl_io_c, gl_st_c = (c // H) % NB, (c // P) % NB
    consts = (e_out, e_state, gl_io_r == gl_io_c, gl_io_r == gl_st_c, gl_st_r == gl_io_c)
    return [jnp.asarray(a, BF16) for a in consts]


def _s5_core_kernel(h_ref, wc_ref, qc_ref, pc_ref, m_ref, d_ref, eo_ref, es_ref, mw_ref, mq_ref, mp_ref,
                    z_ref, carry_scr, b_scr, s_scr, w_scr, q_scr, p_scr, *, rows):
    half = S5_GB * S5_STATE
    strip = 2 * LANES

    @pl.when(pl.program_id(1) == 0)
    def _():
        carry_scr[...] = jnp.zeros_like(carry_scr)
        for c_ref, e_ref, k_ref, dst in ((wc_ref, eo_ref, mw_ref, w_scr), (qc_ref, es_ref, mq_ref, q_scr),
                                         (pc_ref, eo_ref, mp_ref, p_scr)):
            for c0 in range(0, S5_SUB * LANES, strip):
                cs = slice(c0, c0 + strip)
                dst[:, cs] = (_dot(c_ref[0], e_ref[:, cs]) * k_ref[:, cs]).astype(BF16)

    h_t = [h_ref[pl.ds(t, rows, stride=S5_SUB), :] for t in range(S5_SUB)]
    x = jnp.concatenate(h_t, axis=1).astype(BF16)
    b_scr[...] = _dot(x, q_scr[...])

    first_row = lax.broadcasted_iota(jnp.int32, (8, half), 0) == 0
    c_re = carry_scr[:, :half]
    c_im = carry_scr[:, half:]
    for tile in range(rows // 8):
        r0 = tile * 8
        x_re = b_scr[r0:r0 + 8, :half]
        x_im = b_scr[r0:r0 + 8, half:]
        for lvl, sh in enumerate((1, 2, 4)):
            a_re = m_ref[0, 16 * lvl:16 * lvl + 8, :]
            a_im = m_ref[0, 16 * lvl + 8:16 * lvl + 16, :]
            r_re = pltpu.roll(x_re, sh, 0)
            r_im = pltpu.roll(x_im, sh, 0)
            x_re, x_im = (x_re + a_re * r_re - a_im * r_im,
                          x_im + a_re * r_im + a_im * r_re)
        p_re = m_ref[0, 48:56, :]
        p_im = m_ref[0, 56:64, :]
        s_re = x_re + p_re * c_re - p_im * c_im
        s_im = x_im + p_re * c_im + p_im * c_re
        s_scr[r0:r0 + 8, :half] = jnp.where(first_row, c_re, pltpu.roll(s_re, 1, 0))
        s_scr[r0:r0 + 8, half:] = jnp.where(first_row, c_im, pltpu.roll(s_im, 1, 0))
        c_re = jnp.broadcast_to(s_re[7:8, :], (8, half))
        c_im = jnp.broadcast_to(s_im[7:8, :], (8, half))
    carry_scr[:, :half] = c_re
    carry_scr[:, half:] = c_im

    s = s_scr[...].astype(BF16)
    for c0 in range(0, S5_SUB * LANES, strip):
        y = _dot(x[:, :c0 + strip], w_scr[:c0 + strip, c0:c0 + strip]) + _dot(s, p_scr[:, c0:c0 + strip])
        for t in range(c0 // LANES, (c0 + strip) // LANES):
            y_t = y[:, t * LANES - c0:(t + 1) * LANES - c0] + d_ref[...] * h_t[t]
            z_ref[pl.ds(t, rows, stride=S5_SUB), :] = _gelu_tanh(y_t)


def _s5_core(h, w, q, p, mult, d_skip, *, tm):
    L = h.shape[0]
    rows = tm // S5_SUB
    wide = S5_SUB * LANES
    compact = pl.BlockSpec((1, wide, LANES), lambda g, i: (g, 0, 0))
    expand = pl.BlockSpec((LANES, wide), lambda g, i: (0, 0), pipeline_mode=pl.Buffered(1))
    mask = pl.BlockSpec((wide, wide), lambda g, i: (0, 0), pipeline_mode=pl.Buffered(1))
    table = pltpu.VMEM((wide, wide), BF16)
    return pl.pallas_call(
        functools.partial(_s5_core_kernel, rows=rows),
        grid=(S5_GB, L // tm),
        in_specs=[
            pl.BlockSpec((tm, LANES), lambda g, i: (i, g)),
            compact, compact, compact,
            pl.BlockSpec((1, 64, S5_GB * S5_STATE), lambda g, i: (g, 0, 0)),
            pl.BlockSpec((1, LANES), lambda g, i: (0, g)),
            expand, expand, mask, mask, mask,
        ],
        out_specs=pl.BlockSpec((tm, LANES), lambda g, i: (i, g)),
        out_shape=jax.ShapeDtypeStruct((L, D_MODEL), F32),
        scratch_shapes=[pltpu.VMEM((8, wide), F32), pltpu.VMEM((rows, wide), F32),
                        pltpu.VMEM((rows, wide), F32), table, table, table],
        compiler_params=_params("parallel", "arbitrary"),
        name="s5_core",
    )(h, w, q, p, mult, d_skip, *_s5_expanders())


def _s5_mixer_layer(x, g, lam_re, lam_im, log_dt, b_re, b_im, c_re, c_im, d_skip, w_glu, *, tm, tm_core):
    w, q, p, mult = _s5_tables(lam_re, lam_im, log_dt, b_re, b_im, c_re, c_im)
    h = _rms_layer(x, g, tm=tm)
    z = _s5_core(h, w, q, p, mult, d_skip.reshape(1, D_MODEL), tm=tm_core)
    return z, w_glu.astype(BF16)


def _gla_kernel(x_ref, g_ref, win_ref, wa1_ref, wa2_ref, ba_ref, ng_ref,
                wo_ref, tri_ref, o_ref, st_scr, o_scr, upd_scr, stb_scr, *, tm):
    @pl.when(pl.program_id(0) == 0)
    def _():
        st_scr[...] = jnp.zeros_like(st_scr)

    x = x_ref[...]
    h = _rms(x, g_ref[...]).astype(BF16)
    c_k, c_v, c_g = GLA_DK, 2 * GLA_DK, 2 * GLA_DK + GLA_DV
    q = (_dot(h, win_ref[:, :c_k]) * (GLA_DKH ** -0.5)).astype(BF16)
    k = _dot(h, win_ref[:, c_k:c_v])
    v = _dot(h, win_ref[:, c_v:c_g]).astype(BF16)
    gate = _dot(h, win_ref[:, c_g:c_g + GLA_DV])
    a_lo = _dot(h, wa1_ref[...]).astype(BF16)
    log_a = jax.nn.log_sigmoid(_dot(a_lo, wa2_ref[...]) + ba_ref[...]) / GLA_TEMP

    la_hi = log_a.astype(BF16)
    la_lo = (log_a - la_hi.astype(F32)).astype(BF16)
    cum = _dot(tri_ref[...], la_hi) + _dot(tri_ref[...], la_lo)
    n_chunks = tm // CHUNK
    tot_rows = [cum[(c + 1) * CHUNK - 1:(c + 1) * CHUNK, :] for c in range(n_chunks)]
    tot = jnp.concatenate([jnp.broadcast_to(t, (CHUNK, GLA_DK)) for t in tot_rows], axis=0)
    k_dec = (k * jnp.exp(tot - cum)).astype(BF16)

    heads = [(slice(hd * GLA_DKH, (hd + 1) * GLA_DKH), slice(hd * GLA_DVH, (hd + 1) * GLA_DVH))
             for hd in range(GLA_HEADS)]
    for c in range(n_chunks):
        rs = slice(c * CHUNK, (c + 1) * CHUNK)
        for hd, (ks, vs) in enumerate(heads):
            upd_scr[c, hd] = _dot_tn(v[rs, vs], k_dec[rs, ks])
    for c in range(n_chunks):
        dec_row = jnp.exp(tot_rows[c])
        for hd, (ks, vs) in enumerate(heads):
            st = st_scr[hd] * dec_row[:, ks] + upd_scr[c, hd]
            st_scr[hd] = st
            stb_scr[c, hd] = st.astype(BF16)
    for c in range(n_chunks):
        rs = slice(c * CHUNK, (c + 1) * CHUNK)
        for hd, (ks, vs) in enumerate(heads):
            o_scr[rs, vs] = _dot_nt(q[rs, ks], stb_scr[c, hd])

    outs = []
    for hd in range(GLA_HEADS):
        vs = slice(hd * GLA_DVH, (hd + 1) * GLA_DVH)
        o = o_scr[:, vs]
        outs.append(o * lax.rsqrt(jnp.mean(o * o, axis=-1, keepdims=True) + NORM_EPS) * ng_ref[:, vs])
    o = jnp.concatenate(outs, axis=1) * (gate * jax.nn.sigmoid(gate))
    o_ref[...] = x + _dot(o.astype(BF16), wo_ref[...])


def _gla_mixer_layer(x, g, w_in_main, w_a1, w_a2, b_a, norm_g, w_o, *, tm):
    L = x.shape[0]
    wa1 = jnp.pad(w_a1.astype(BF16), ((0, 0), (0, LANES - GLA_GATE_RANK)))
    wa2 = jnp.pad(w_a2.astype(BF16), ((0, LANES - GLA_GATE_RANK), (0, 0)))
    r = jnp.arange(tm)
    tri = (((r[:, None] // CHUNK) == (r[None, :] // CHUNK)) & (r[None, :] <= r[:, None])).astype(BF16)
    full = _resident
    n_chunks = tm // CHUNK

    row = pl.BlockSpec((tm, D_MODEL), lambda i: (i, 0))
    return pl.pallas_call(
        functools.partial(_gla_kernel, tm=tm),
        grid=(L // tm,),
        in_specs=[row, full((1, D_MODEL)), full((D_MODEL, 2 * GLA_DK + 2 * GLA_DV)),
                  full((D_MODEL, LANES)), full((LANES, GLA_DK)), full((1, GLA_DK)), full((1, GLA_DV)),
                  full((GLA_DV, D_MODEL)), full((tm, tm))],
        out_specs=row,
        out_shape=jax.ShapeDtypeStruct((L, D_MODEL), F32),
        scratch_shapes=[pltpu.VMEM((GLA_HEADS, GLA_DVH, GLA_DKH), F32),
                        pltpu.VMEM((tm, GLA_DV), F32),
                        pltpu.VMEM((n_chunks, GLA_HEADS, GLA_DVH, GLA_DKH), F32),
                        pltpu.VMEM((n_chunks, GLA_HEADS, GLA_DVH, GLA_DKH), BF16)],
        compiler_params=_params("arbitrary"),
        name="gla",
    )(x, g, w_in_main, wa1, wa2, b_a.reshape(1, GLA_DK), norm_g.reshape(1, GLA_DV),
      w_o.astype(BF16), tri)


def _rope_tables(positions):
    half = ROT_DIMS // 2
    inv_freq = ROPE_THETA ** (-jnp.arange(half, dtype=F32) / half)
    ang = positions.astype(F32)[:, None] * inv_freq
    cos = jnp.cos(ang)
    sin = jnp.sin(ang)
    m = jnp.arange(LANES) % DIFF_HD
    cos_l = cos[:, m % half]
    sin_l = sin[:, m % half]
    c = jnp.where(m < ROT_DIMS, cos_l, 1.0)
    s_lo = jnp.where(m < half, -sin_l, 0.0)
    s_hi = jnp.where((m >= half) & (m < ROT_DIMS), sin_l, 0.0)
    return c, s_lo, s_hi


def _qkv_kernel(x_ref, g_ref, w_ref, c_ref, slo_ref, shi_ref, qt_ref, k_ref, vt_ref):
    h = _rms(x_ref[...], g_ref[...]).astype(BF16)
    qkv = _dot(h, w_ref[...])
    c = c_ref[...]
    s_lo = slo_ref[...]
    s_hi = shi_ref[...]
    half = ROT_DIMS // 2

    def rope(t):
        return t * c + pltpu.roll(t, LANES - half, 1) * s_lo + pltpu.roll(t, half, 1) * s_hi

    for j in range(D_MODEL // LANES):
        ls = slice(j * LANES, (j + 1) * LANES)
        q = rope(qkv[:, ls]) * (DIFF_HD ** -0.5 * LOG2E)
        qt_ref[ls, :] = q.T.astype(BF16)
        k_ref[j, 0] = rope(qkv[:, D_MODEL + j * LANES:D_MODEL + (j + 1) * LANES]).astype(BF16)
        vt_ref[j, 0] = qkv[:, 2 * D_MODEL + j * LANES:2 * D_MODEL + (j + 1) * LANES].T.astype(BF16)


def _qkv_layer(x, g, w_qkv, c, s_lo, s_hi, *, tm):
    L = x.shape[0]
    n = L // tm
    row = pl.BlockSpec((tm, D_MODEL), lambda i: (i, 0))
    tab = pl.BlockSpec((tm, LANES), lambda i: (i, 0))
    return pl.pallas_call(
        _qkv_kernel,
        grid=(n,),
        in_specs=[row, pl.BlockSpec((1, D_MODEL), lambda i: (0, 0)),
                  _resident((D_MODEL, 3 * D_MODEL)), tab, tab, tab],
        out_specs=[pl.BlockSpec((D_MODEL, tm), lambda i: (0, i)),
                   pl.BlockSpec((DIFF_HEADS, 1, tm, LANES), lambda i: (0, i, 0, 0)),
                   pl.BlockSpec((DIFF_HEADS, 1, LANES, tm), lambda i: (0, i, 0, 0))],
        out_shape=[jax.ShapeDtypeStruct((D_MODEL, L), BF16),
                   jax.ShapeDtypeStruct((DIFF_HEADS, n, tm, LANES), BF16),
                   jax.ShapeDtypeStruct((DIFF_HEADS, n, LANES, tm), BF16)],
        compiler_params=_params("parallel"),
        name="diff_qkv",
    )(x, g, w_qkv, c, s_lo, s_hi)


def _attn_kernel(qt_ref, k_ref, vt_ref, lamv_ref, sg_ref, bias_ref, o_ref, m_scr, l_scr, acc_scr,
                 sa_scr, sb_scr, ma_scr, mb_scr, *, lambda_init):
    qi = pl.program_id(1)
    i = qi // 2
    qt = qt_ref[...]
    row = lax.broadcasted_iota(jnp.int32, qt.shape, 0)
    zero = jnp.zeros_like(qt)
    q_half = (jnp.where(row < DIFF_HD, qt, zero), jnp.where(row >= DIFF_HD, qt, zero))

    m_scr[...] = jnp.full_like(m_scr, NEG_INF)
    l_scr[...] = jnp.zeros_like(l_scr)
    acc_scr[...] = jnp.zeros_like(acc_scr)

    tq = qt.shape[1]

    def scores(j, slot, part=None, masked=False):
        s_ref, smax_ref = slot
        rows = slice(None) if part is None else slice(part * tq, (part + 1) * tq)
        k = k_ref[0, j, rows, :]
        for hf in range(2):
            s = _dot(k, q_half[hf])
            if masked:
                s = s + bias_ref[...]
            if part is None:
                s_ref[hf] = s
            else:
                s_ref[hf, :tq, :] = s
            smax_ref[hf] = jnp.max(s, axis=0, keepdims=True)

    def update(j, slot, part=None):
        s_ref, smax_ref = slot
        cols = slice(None) if part is None else slice(part * tq, (part + 1) * tq)
        vt = vt_ref[0, j, :, cols]
        for hf in range(2):
            m_prev = m_scr[hf]
            m_new = jnp.maximum(m_prev, smax_ref[hf])
            alpha = jnp.exp2(m_prev - m_new)
            s = s_ref[hf] if part is None else s_ref[hf, :tq, :]
            p = jnp.exp2(s - m_new)
            l_scr[hf] = alpha * l_scr[hf] + jnp.sum(p, axis=0, keepdims=True)
            acc_scr[hf] = alpha * acc_scr[hf] + _dot(vt, p.astype(BF16))
            m_scr[hf] = m_new

    slot_a, slot_b = (sa_scr, ma_scr), (sb_scr, mb_scr)

    def pair(j):
        scores(j + 1, slot_b)
        update(j, slot_a)
        scores(j + 2, slot_a)
        update(j + 1, slot_b)

    unroll = 4

    def trip(t, carry):
        for u in range(unroll):
            pair(2 * (unroll * t + u))
        return carry

    def diagonal(cond, prefix, pending, free, jp):
        @pl.when(cond & (qi % 2 == 0))
        def _():
            prefix()
            scores(i, free, part=0, masked=True)
            if jp is not None:
                update(jp, pending)
            update(i, free, part=0)

        @pl.when(cond & (qi % 2 == 1))
        def _():
            prefix()
            scores(i, free, part=0)
            if jp is not None:
                update(jp, pending)
            scores(i, pending, part=1, masked=True)
            update(i, free, part=0)
            update(i, pending, part=1)

    diagonal(i == 0, lambda: None, slot_b, slot_a, None)

    @pl.when(i > 0)
    def _():
        scores(0, slot_a)

    n_pairs = jnp.maximum(i - 1, 0) // 2
    lax.fori_loop(0, n_pairs // unroll, trip, 0)
    done = (n_pairs // unroll) * unroll
    for width in (2, 1):
        @pl.when((n_pairs // width) % 2 == 1)
        def _(width=width, done=done):
            for u in range(width):
                pair(2 * (done + u))
        done = done + jnp.where((n_pairs // width) % 2 == 1, width, 0)

    diagonal(i % 2 == 1, lambda: None, slot_a, slot_b, i - 1)

    def two_left():
        scores(i - 1, slot_b)
        update(i - 2, slot_a)

    diagonal((i % 2 == 0) & (i > 0), two_left, slot_b, slot_a, i - 1)

    lv = lamv_ref[...]
    lam = (jnp.exp(jnp.sum(lv[0:1] * lv[1:2], axis=-1, keepdims=True))
           - jnp.exp(jnp.sum(lv[2:3] * lv[3:4], axis=-1, keepdims=True)) + lambda_init)
    o = acc_scr[0] / l_scr[0] - lam * (acc_scr[1] / l_scr[1])
    o = o * lax.rsqrt(jnp.mean(o * o, axis=0, keepdims=True) + SUBLN_EPS) * sg_ref[...]
    o_ref[...] = (o * (1.0 - lambda_init)).astype(BF16)


def _attn_layer(qt, k, vt, lamv, subln_g, *, tq, lambda_init):
    n, tk = k.shape[1], k.shape[2]
    L = n * tk
    assert tk == 2 * tq, (tk, tq)
    stat = pltpu.VMEM((2, 1, tq), F32)
    chunk = np.arange(tq) // CHUNK
    bias = np.where(chunk[:, None] <= chunk[None, :], 0.0, NEG_INF)
    return pl.pallas_call(
        functools.partial(_attn_kernel, lambda_init=lambda_init),
        grid=(DIFF_HEADS, L // tq),
        in_specs=[
            pl.BlockSpec((LANES, tq), lambda p, i: (p, i)),
            pl.BlockSpec((1, n, tk, LANES), lambda p, i: (p, 0, 0, 0)),
            pl.BlockSpec((1, n, LANES, tk), lambda p, i: (p, 0, 0, 0)),
            pl.BlockSpec((4, DIFF_HD), lambda p, i: (0, 0)),
            pl.BlockSpec((LANES, 1), lambda p, i: (0, 0)),
            pl.BlockSpec((tq, tq), lambda p, i: (0, 0), pipeline_mode=pl.Buffered(1)),
        ],
        out_specs=pl.BlockSpec((LANES, tq), lambda p, i: (p, i)),
        out_shape=jax.ShapeDtypeStruct((D_MODEL, L), BF16),
        scratch_shapes=[stat, stat, pltpu.VMEM((2, LANES, tq), F32),
                        pltpu.VMEM((2, tk, tq), F32), pltpu.VMEM((2, tk, tq), F32), stat, stat],
        compiler_params=_params("parallel", "arbitrary"),
        name="diff_attn",
    )(qt, k, vt, lamv, subln_g, jnp.asarray(bias, F32))


def _diff_mixer_layer(x, g, positions, w_qkv, lam_q1, lam_k1, lam_q2, lam_k2, subln_g, w_o,
                      lambda_init, *, tq, tk):
    c, s_lo, s_hi = _rope_tables(positions)
    qt, k4, vt4 = _qkv_layer(x, g, w_qkv.astype(BF16), c, s_lo, s_hi, tm=tk)
    lamv = jnp.stack([lam_q1, lam_k1, lam_q2, lam_k2]).astype(F32)
    at = _attn_layer(qt, k4, vt4, lamv, subln_g.reshape(LANES, 1).astype(F32), tq=tq,
                     lambda_init=lambda_init)
    return at, w_o.astype(BF16)


def _trunk(x, positions, norm_mix, norm_ffn, norm_final,
           s5_lam_re, s5_lam_im, s5_log_dt, s5_b_re, s5_b_im, s5_c_re, s5_c_im, s5_d, s5_w_glu,
           gla_w_in, gla_w_a2, gla_b_a, gla_norm, gla_w_o,
           diff_w_qkv, diff_lam_q1, diff_lam_k1, diff_lam_q2, diff_lam_k2, diff_subln, diff_w_o,
           ffn_w_gate_up, ffn_w_down, *, tm, tm_core, tm_gla, tq, tk, tm_ffn, th):
    depth = norm_mix.shape[0]
    gfinal = norm_final.reshape(1, D_MODEL)
    gla_w_a1 = gla_w_in[:, :, 2 * GLA_DK + 2 * GLA_DV:]
    gla_w_in = _to_bf16(gla_w_in, cols=2 * GLA_DK + 2 * GLA_DV)
    s5_w_glu, gla_w_o, diff_w_qkv, diff_w_o, ffn_w_gate_up, ffn_w_down = (
        _to_bf16(w) for w in (s5_w_glu, gla_w_o, diff_w_qkv, diff_w_o, ffn_w_gate_up, ffn_w_down))
    for layer in range(depth):
        kind = layer % N_MIXERS
        idx = layer // N_MIXERS
        g = norm_mix[layer].reshape(1, D_MODEL)
        if kind == 0:
            pre, tm_f = "s5", tm
            pre_args = _s5_mixer_layer(x, g, s5_lam_re[idx], s5_lam_im[idx], s5_log_dt[idx],
                                       s5_b_re[idx], s5_b_im[idx], s5_c_re[idx], s5_c_im[idx],
                                       s5_d[idx], s5_w_glu[idx], tm=tm, tm_core=tm_core)
        elif kind == 1:
            pre, tm_f, pre_args = None, tm_ffn, ()
            x = _gla_mixer_layer(x, g, gla_w_in[idx], gla_w_a1[idx], gla_w_a2[idx], gla_b_a[idx],
                                 gla_norm[idx], gla_w_o[idx], tm=tm_gla)
        else:
            pre, tm_f = "proj", tm
            lambda_init = 0.8 - 0.6 * math.exp(-0.3 * layer)
            pre_args = _diff_mixer_layer(x, g, positions, diff_w_qkv[idx], diff_lam_q1[idx],
                                         diff_lam_k1[idx], diff_lam_q2[idx], diff_lam_k2[idx],
                                         diff_subln[idx], diff_w_o[idx], lambda_init, tq=tq, tk=tk)
        x = _ffn_layer(x, pre_args, norm_ffn[layer].reshape(1, D_MODEL), ffn_w_gate_up, ffn_w_down,
                       gfinal, layer=layer, final=(layer == depth - 1), tm=tm_f, th=th, pre=pre)
    return x


def kernel(x, positions, norm_mix, norm_ffn, norm_final, s5_lam_re, s5_lam_im, s5_log_dt, s5_b_re, s5_b_im, s5_c_re, s5_c_im, s5_d, s5_w_glu, gla_w_in, gla_w_a2, gla_b_a, gla_norm, gla_w_o, diff_w_qkv, diff_lam_q1, diff_lam_k1, diff_lam_q2, diff_lam_k2, diff_subln, diff_w_o, ffn_w_gate_up, ffn_w_down):
    bsz, seq, _ = x.shape
    outs = []
    for b in range(bsz):
        outs.append(_trunk(
            x[b], positions[b], norm_mix, norm_ffn, norm_final,
            s5_lam_re, s5_lam_im, s5_log_dt, s5_b_re, s5_b_im, s5_c_re, s5_c_im, s5_d, s5_w_glu,
            gla_w_in, gla_w_a2, gla_b_a, gla_norm, gla_w_o,
            diff_w_qkv, diff_lam_q1, diff_lam_k1, diff_lam_q2, diff_lam_k2, diff_subln, diff_w_o,
            ffn_w_gate_up, ffn_w_down,
            tm=512, tm_core=4096, tm_gla=512, tq=512, tk=1024, tm_ffn=1024, th=256))
    return jnp.stack(outs)
```

```python
import functools
import math

import jax
import jax.numpy as jnp
import numpy as np
from jax import lax
from jax.experimental import pallas as pl
from jax.experimental.pallas import tpu as pltpu

F32 = jnp.float32
BF16 = jnp.bfloat16

D_MODEL = 1024
DEPTH = 4
CHUNK = 64
N_MIXERS = 3
NORM_EPS = 1e-6
S5_GROUP = 16
S5_GROUPS = D_MODEL // S5_GROUP
S5_STATE = 64
S5_SUB = 8
S5_GB = 8
GLA_HEADS = 4
GLA_DK = D_MODEL // 2
GLA_DV = D_MODEL
GLA_DKH = GLA_DK // GLA_HEADS
GLA_DVH = GLA_DV // GLA_HEADS
GLA_GATE_RANK = 16
GLA_TEMP = 16.0
DIFF_HD = 64
DIFF_HEADS = D_MODEL // (2 * DIFF_HD)
ROT_DIMS = DIFF_HD // 4
ROPE_THETA = 500000.0
NEG_INF = -1e30
SUBLN_EPS = 1e-5
LOG2E = math.log2(math.e)
FFN_HIDDEN = -(-8 * D_MODEL // (3 * 256)) * 256

LANES = 128
VMEM_LIMIT_BYTES = 56 * 1024 * 1024


def _params(*sem):
    return pltpu.CompilerParams(dimension_semantics=sem, vmem_limit_bytes=VMEM_LIMIT_BYTES)


def _rms(x, g, eps=NORM_EPS):
    return x * lax.rsqrt(jnp.mean(x * x, axis=-1, keepdims=True) + eps) * g


def _dot(a, b):
    return jnp.dot(a, b, preferred_element_type=F32)


def _dot_nt(a, b):
    return lax.dot_general(a, b, (((1,), (1,)), ((), ())), preferred_element_type=F32)


def _dot_tn(a, b):
    return lax.dot_general(a, b, (((0,), (0,)), ((), ())), preferred_element_type=F32)


def _cast_kernel(w_ref, o_ref):
    o_ref[...] = w_ref[...].astype(BF16)


def _to_bf16(w, cols=None):
    n_layers, rows, full = w.shape
    cols = full if cols is None else cols
    row_blocks = 4
    spec = pl.BlockSpec((1, rows // row_blocks, cols), lambda l, r: (l, r, 0))
    return pl.pallas_call(
        _cast_kernel,
        grid=(n_layers, row_blocks),
        in_specs=[spec],
        out_specs=spec,
        out_shape=jax.ShapeDtypeStruct((n_layers, rows, cols), BF16),
        compiler_params=_params("parallel", "parallel"),
        name="to_bf16",
    )(w)


def _gelu_tanh(x):
    return x * (0.5 * (1.0 + jnp.tanh(math.sqrt(2.0 / math.pi) * (x + 0.044715 * (x * x * x)))))


def _ffn_kernel(*refs, th, final, pre):
    if pre == "s5":
        x_ref, z_ref, wglu_ref, g_ref, wgu_ref, wd_ref, gf_ref, o_ref, h_scr = refs
        ab = _dot(z_ref[...].astype(BF16), wglu_ref[...])
        x = x_ref[...] + ab[:, :D_MODEL] * jax.nn.sigmoid(ab[:, D_MODEL:])
    elif pre == "proj":
        x_ref, at_ref, wo_ref, g_ref, wgu_ref, wd_ref, gf_ref, o_ref, h_scr = refs
        x = x_ref[...] + _dot_tn(at_ref[...], wo_ref[...])
    else:
        x_ref, g_ref, wgu_ref, wd_ref, gf_ref, o_ref, h_scr = refs
        x = x_ref[...]
    h_scr[...] = _rms(x, g_ref[...]).astype(BF16)
    o_ref[...] = x
    for c in range(FFN_HIDDEN // th):
        lo, hi = c * th, (c + 1) * th
        gate = _dot(h_scr[...], wgu_ref[0, :, lo:hi])
        up = _dot(h_scr[...], wgu_ref[0, :, FFN_HIDDEN + lo:FFN_HIDDEN + hi])
        act = (gate * jax.nn.sigmoid(gate) * up).astype(BF16)
        o_ref[...] += _dot(act, wd_ref[0, lo:hi, :])
    if final:
        o_ref[...] = _rms(o_ref[...], gf_ref[...])


def _resident(shape):
    return pl.BlockSpec(shape, lambda i: (0,) * len(shape), pipeline_mode=pl.Buffered(1))


def _ffn_layer(x, pre_args, g, w_gate_up, w_down, gf, *, layer, final, tm, th, pre=None):
    L = x.shape[0]
    row = pl.BlockSpec((tm, D_MODEL), lambda i: (i, 0))
    vec = pl.BlockSpec((1, D_MODEL), lambda i: (0, 0))

    def slab(rows, cols):
        return pl.BlockSpec((1, rows, cols), lambda i: (layer, 0, 0), pipeline_mode=pl.Buffered(1))

    if pre == "s5":
        pre_specs = [row, _resident((D_MODEL, 2 * D_MODEL))]
    elif pre == "proj":
        pre_specs = [pl.BlockSpec((D_MODEL, tm), lambda i: (0, i)), _resident((D_MODEL, D_MODEL))]
    else:
        pre_specs = []
    return pl.pallas_call(
        functools.partial(_ffn_kernel, th=th, final=final, pre=pre),
        grid=(L // tm,),
        in_specs=[row] + pre_specs + [vec, slab(D_MODEL, 2 * FFN_HIDDEN), slab(FFN_HIDDEN, D_MODEL), vec],
        out_specs=row,
        out_shape=jax.ShapeDtypeStruct((L, D_MODEL), F32),
        scratch_shapes=[pltpu.VMEM((tm, D_MODEL), BF16)],
        compiler_params=_params("parallel"),
        name="ffn" if pre is None else "ffn_" + pre,
    )(x, *pre_args, g, w_gate_up, w_down, gf)


def _s5_tables(lam_re, lam_im, log_dt, b_re, b_im, c_re, c_im):
    G, P, H, T, NB = S5_GROUPS, S5_STATE, S5_GROUP, S5_SUB, S5_GB
    f32 = F32
    lr = lam_re.astype(f32)
    li = lam_im.astype(f32)
    dt = jnp.exp(log_dt.astype(f32))[:, None]
    ab_mag = jnp.exp(lr * dt)
    ab_ang = li * dt
    ab_re = ab_mag * jnp.cos(ab_ang)
    ab_im = ab_mag * jnp.sin(ab_ang)
    den = lr * lr + li * li
    f_re = ((ab_re - 1.0) * lr + ab_im * li) / den
    f_im = (ab_im * lr - (ab_re - 1.0) * li) / den
    br = b_re.astype(f32)
    bi = b_im.astype(f32)
    bb_re = f_re[..., None] * br - f_im[..., None] * bi
    bb_im = f_re[..., None] * bi + f_im[..., None] * br
    cr = c_re.astype(f32)
    ci = c_im.astype(f32)

    def powers(n):
        n = n.astype(f32)[:, None, None]
        mag = jnp.exp(lr * dt * n)
        ang = li * dt * n
        return mag * jnp.cos(ang), mag * jnp.sin(ang)

    pr, pi = powers(jnp.arange(T + 1))
    cp_re = cr[None] * pr[:, :, None, :] - ci[None] * pi[:, :, None, :]
    cp_im = cr[None] * pi[:, :, None, :] + ci[None] * pr[:, :, None, :]
    cpt_re = cp_re[:T].transpose(0, 2, 3, 1)[:, :, :, None, :]
    cpt_im = cp_im[:T].transpose(0, 2, 3, 1)[:, :, :, None, :]
    bbt_re = bb_re.transpose(1, 2, 0)[None, None]
    bbt_im = bb_im.transpose(1, 2, 0)[None, None]
    kern = jnp.sum(cpt_re * bbt_re - cpt_im * bbt_im, axis=2)
    kern = kern.transpose(0, 3, 1, 2)

    wide = T * NB * H
    t_in = jnp.arange(T)[:, None]
    t_out = jnp.arange(T)[None, :]
    lag = jnp.clip(t_out - t_in, 0, T - 1)
    causal = (t_out >= t_in).astype(f32)
    xk = kern.transpose(1, 0, 3, 2).reshape(NB, NB, T, H, H)
    xk = xk.transpose(0, 2, 1, 3, 4).reshape(NB, T, NB * H, H)
    xk = xk[:, lag] * causal[None, :, :, None, None]
    w = xk.transpose(0, 1, 3, 2, 4).reshape(NB, wide, T * H)

    prq = pr[T - 1 - jnp.arange(T)]
    piq = pi[T - 1 - jnp.arange(T)]
    qv_re = prq[..., None] * bb_re[None] - piq[..., None] * bb_im[None]
    qv_im = prq[..., None] * bb_im[None] + piq[..., None] * bb_re[None]
    qv = jnp.stack([qv_re, qv_im]).reshape(2, T, NB, NB, P, H)
    q = qv.transpose(2, 1, 3, 5, 0, 4).reshape(NB, wide, 2 * P)

    pm = jnp.stack([cp_re[1:], -cp_im[1:]]).reshape(2, T, NB, NB, H, P)
    p = pm.transpose(2, 0, 3, 5, 1, 4).reshape(NB, wide, T * H)
    w, q, p = w.astype(BF16), q.astype(BF16), p.astype(BF16)

    sr, si = powers(T * jnp.arange(1, 9))
    rows = jnp.arange(8)
    kinds = []
    for sh in (1, 2, 4):
        m = (rows >= sh).astype(f32)[:, None, None]
        kinds += [m * sr[sh - 1][None], m * si[sh - 1][None]]
    kinds += [sr, si]
    mult = jnp.stack(kinds)
    mult = mult.reshape(8, 8, NB, NB * P).transpose(2, 0, 1, 3).reshape(NB, 64, NB * P)
    return w, q, p, mult


def _rms_kernel(x_ref, g_ref, o_ref):
    o_ref[...] = _rms(x_ref[...], g_ref[...])


def _rms_layer(x, g, *, tm):
    L = x.shape[0]
    return pl.pallas_call(
        _rms_kernel,
        grid=(L // tm,),
        in_specs=[pl.BlockSpec((tm, D_MODEL), lambda i: (i, 0)),
                  pl.BlockSpec((1, D_MODEL), lambda i: (0, 0))],
        out_specs=pl.BlockSpec((tm, D_MODEL), lambda i: (i, 0)),
        out_shape=jax.ShapeDtypeStruct((L, D_MODEL), F32),
        compiler_params=_params("parallel"),
        name="s5_norm",
    )(x, g)


def _s5_expanders():
    T, NB, H, P = S5_SUB, S5_GB, S5_GROUP, S5_STATE
    wide = T * NB * H
    r = np.arange(LANES)[:, None]
    c = np.arange(wide)[None, :]
    rr = np.arange(wide)[:, None]
    e_out = (r // H == c // (NB * H)) & (r % H == c % H)
    e_state = (r // P == c // (NB * P)) & (r % P == c % P)
    gl_io_r, gl_st_r = (rr // H) % NB, (rr // P) % NB
    gl_io_c, gl_st_c = (c // H) % NB, (c // P) % NB
    consts = (e_out, e_state, gl_io_r == gl_io_c, gl_io_r == gl_st_c, gl_st_r == gl_io_c)
    return [jnp.asarray(a, BF16) for a in consts]


def _s5_core_kernel(h_ref, wc_ref, qc_ref, pc_ref, m_ref, d_ref, eo_ref, es_ref, mw_ref, mq_ref, mp_ref,
                    z_ref, carry_scr, b_scr, s_scr, w_scr, q_scr, p_scr, *, rows):
    half = S5_GB * S5_STATE
    strip = 2 * LANES

    @pl.when(pl.program_id(1) == 0)
    def _():
        carry_scr[...] = jnp.zeros_like(carry_scr)
        for c_ref, e_ref, k_ref, dst in ((wc_ref, eo_ref, mw_ref, w_scr), (qc_ref, es_ref, mq_ref, q_scr),
                                         (pc_ref, eo_ref, mp_ref, p_scr)):
            for c0 in range(0, S5_SUB * LANES, strip):
                cs = slice(c0, c0 + strip)
                dst[:, cs] = (_dot(c_ref[0], e_ref[:, cs]) * k_ref[:, cs]).astype(BF16)

    h_t = [h_ref[pl.ds(t, rows, stride=S5_SUB), :] for t in range(S5_SUB)]
    x = jnp.concatenate(h_t, axis=1).astype(BF16)
    b_scr[...] = _dot(x, q_scr[...])

    first_row = lax.broadcasted_iota(jnp.int32, (8, half), 0) == 0
    c_re = carry_scr[:, :half]
    c_im = carry_scr[:, half:]
    for tile in range(rows // 8):
        r0 = tile * 8
        x_re = b_scr[r0:r0 + 8, :half]
        x_im = b_scr[r0:r0 + 8, half:]
        for lvl, sh in enumerate((1, 2, 4)):
            a_re = m_ref[0, 16 * lvl:16 * lvl + 8, :]
            a_im = m_ref[0, 16 * lvl + 8:16 * lvl + 16, :]
            r_re = pltpu.roll(x_re, sh, 0)
            r_im = pltpu.roll(x_im, sh, 0)
            x_re, x_im = (x_re + a_re * r_re - a_im * r_im,
                          x_im + a_re * r_im + a_im * r_re)
        p_re = m_ref[0, 48:56, :]
        p_im = m_ref[0, 56:64, :]
        s_re = x_re + p_re * c_re - p_im * c_im
        s_im = x_im + p_re * c_im + p_im * c_re
        s_scr[r0:r0 + 8, :half] = jnp.where(first_row, c_re, pltpu.roll(s_re, 1, 0))
        s_scr[r0:r0 + 8, half:] = jnp.where(first_row, c_im, pltpu.roll(s_im, 1, 0))
        c_re = jnp.broadcast_to(s_re[7:8, :], (8, half))
        c_im = jnp.broadcast_to(s_im[7:8, :], (8, half))
    carry_scr[:, :half] = c_re
    carry_scr[:, half:] = c_im

    s = s_scr[...].astype(BF16)
    for c0 in range(0, S5_SUB * LANES, strip):
        y = _dot(x[:, :c0 + strip], w_scr[:c0 + strip, c0:c0 + strip]) + _dot(s, p_scr[:, c0:c0 + strip])
        for t in range(c0 // LANES, (c0 + strip) // LANES):
            y_t = y[:, t * LANES - c0:(t + 1) * LANES - c0] + d_ref[...] * h_t[t]
            z_ref[pl.ds(t, rows, stride=S5_SUB), :] = _gelu_tanh(y_t)


def _s5_core(h, w, q, p, mult, d_skip, *, tm):
    L = h.shape[0]
    rows = tm // S5_SUB
    wide = S5_SUB * LANES
    compact = pl.BlockSpec((1, wide, LANES), lambda g, i: (g, 0, 0))
    expand = pl.BlockSpec((LANES, wide), lambda g, i: (0, 0), pipeline_mode=pl.Buffered(1))
    mask = pl.BlockSpec((wide, wide), lambda g, i: (0, 0), pipeline_mode=pl.Buffered(1))
    table = pltpu.VMEM((wide, wide), BF16)
    return pl.pallas_call(
        functools.partial(_s5_core_kernel, rows=rows),
        grid=(S5_GB, L // tm),
        in_specs=[
            pl.BlockSpec((tm, LANES), lambda g, i: (i, g)),
            compact, compact, compact,
            pl.BlockSpec((1, 64, S5_GB * S5_STATE), lambda g, i: (g, 0, 0)),
            pl.BlockSpec((1, LANES), lambda g, i: (0, g)),
            expand, expand, mask, mask, mask,
        ],
        out_specs=pl.BlockSpec((tm, LANES), lambda g, i: (i, g)),
        out_shape=jax.ShapeDtypeStruct((L, D_MODEL), F32),
        scratch_shapes=[pltpu.VMEM((8, wide), F32), pltpu.VMEM((rows, wide), F32),
                        pltpu.VMEM((rows, wide), F32), table, table, table],
        compiler_params=_params("parallel", "arbitrary"),
        name="s5_core",
    )(h, w, q, p, mult, d_skip, *_s5_expanders())


def _s5_mixer_layer(x, g, lam_re, lam_im, log_dt, b_re, b_im, c_re, c_im, d_skip, w_glu, *, tm, tm_core):
    w, q, p, mult = _s5_tables(lam_re, lam_im, log_dt, b_re, b_im, c_re, c_im)
    h = _rms_layer(x, g, tm=tm)
    z = _s5_core(h, w, q, p, mult, d_skip.reshape(1, D_MODEL), tm=tm_core)
    return z, w_glu.astype(BF16)


def _gla_kernel(x_ref, g_ref, win_ref, wa1_ref, wa2_ref, ba_ref, ng_ref,
                wo_ref, tri_ref, o_ref, st_scr, o_scr, upd_scr, stb_scr, *, tm):
    @pl.when(pl.program_id(0) == 0)
    def _():
        st_scr[...] = jnp.zeros_like(st_scr)

    x = x_ref[...]
    h = _rms(x, g_ref[...]).astype(BF16)
    c_k, c_v, c_g = GLA_DK, 2 * GLA_DK, 2 * GLA_DK + GLA_DV
    q = (_dot(h, win_ref[:, :c_k]) * (GLA_DKH ** -0.5)).astype(BF16)
    k = _dot(h, win_ref[:, c_k:c_v])
    v = _dot(h, win_ref[:, c_v:c_g]).astype(BF16)
    gate = _dot(h, win_ref[:, c_g:c_g + GLA_DV])
    a_lo = _dot(h, wa1_ref[...]).astype(BF16)
    log_a = jax.nn.log_sigmoid(_dot(a_lo, wa2_ref[...]) + ba_ref[...]) / GLA_TEMP

    la_hi = log_a.astype(BF16)
    la_lo = (log_a - la_hi.astype(F32)).astype(BF16)
    cum = _dot(tri_ref[...], la_hi) + _dot(tri_ref[...], la_lo)
    n_chunks = tm // CHUNK
    tot_rows = [cum[(c + 1) * CHUNK - 1:(c + 1) * CHUNK, :] for c in range(n_chunks)]
    tot = jnp.concatenate([jnp.broadcast_to(t, (CHUNK, GLA_DK)) for t in tot_rows], axis=0)
    k_dec = (k * jnp.exp(tot - cum)).astype(BF16)

    heads = [(slice(hd * GLA_DKH, (hd + 1) * GLA_DKH), slice(hd * GLA_DVH, (hd + 1) * GLA_DVH))
             for hd in range(GLA_HEADS)]
    for c in range(n_chunks):
        rs = slice(c * CHUNK, (c + 1) * CHUNK)
        for hd, (ks, vs) in enumerate(heads):
            upd_scr[c, hd] = _dot_tn(v[rs, vs], k_dec[rs, ks])
    for c in range(n_chunks):
        dec_row = jnp.exp(tot_rows[c])
        for hd, (ks, vs) in enumerate(heads):
            st = st_scr[hd] * dec_row[:, ks] + upd_scr[c, hd]
            st_scr[hd] = st
            stb_scr[c, hd] = st.astype(BF16)
    for c in range(n_chunks):
        rs = slice(c * CHUNK, (c + 1) * CHUNK)
        for hd, (ks, vs) in enumerate(heads):
            o_scr[rs, vs] = _dot_nt(q[rs, ks], stb_scr[c, hd])

    outs = []
    for hd in range(GLA_HEADS):
        vs = slice(hd * GLA_DVH, (hd + 1) * GLA_DVH)
        o = o_scr[:, vs]
        outs.append(o * lax.rsqrt(jnp.mean(o * o, axis=-1, keepdims=True) + NORM_EPS) * ng_ref[:, vs])
    o = jnp.concatenate(outs, axis=1) * (gate * jax.nn.sigmoid(gate))
    o_ref[...] = x + _dot(o.astype(BF16), wo_ref[...])


def _gla_mixer_layer(x, g, w_in_main, w_a1, w_a2, b_a, norm_g, w_o, *, tm):
    L = x.shape[0]
    wa1 = jnp.pad(w_a1.astype(BF16), ((0, 0), (0, LANES - GLA_GATE_RANK)))
    wa2 = jnp.pad(w_a2.astype(BF16), ((0, LANES - GLA_GATE_RANK), (0, 0)))
    r = jnp.arange(tm)
    tri = (((r[:, None] // CHUNK) == (r[None, :] // CHUNK)) & (r[None, :] <= r[:, None])).astype(BF16)
    full = _resident
    n_chunks = tm // CHUNK

    row = pl.BlockSpec((tm, D_MODEL), lambda i: (i, 0))
    return pl.pallas_call(
        functools.partial(_gla_kernel, tm=tm),
        grid=(L // tm,),
        in_specs=[row, full((1, D_MODEL)), full((D_MODEL, 2 * GLA_DK + 2 * GLA_DV)),
                  full((D_MODEL, LANES)), full((LANES, GLA_DK)), full((1, GLA_DK)), full((1, GLA_DV)),
                  full((GLA_DV, D_MODEL)), full((tm, tm))],
        out_specs=row,
        out_shape=jax.ShapeDtypeStruct((L, D_MODEL), F32),
        scratch_shapes=[pltpu.VMEM((GLA_HEADS, GLA_DVH, GLA_DKH), F32),
                        pltpu.VMEM((tm, GLA_DV), F32),
                        pltpu.VMEM((n_chunks, GLA_HEADS, GLA_DVH, GLA_DKH), F32),
                        pltpu.VMEM((n_chunks, GLA_HEADS, GLA_DVH, GLA_DKH), BF16)],
        compiler_params=_params("arbitrary"),
        name="gla",
    )(x, g, w_in_main, wa1, wa2, b_a.reshape(1, GLA_DK), norm_g.reshape(1, GLA_DV),
      w_o.astype(BF16), tri)


def _rope_tables(positions):
    half = ROT_DIMS // 2
    inv_freq = ROPE_THETA ** (-jnp.arange(half, dtype=F32) / half)
    ang = positions.astype(F32)[:, None] * inv_freq
    cos = jnp.cos(ang)
    sin = jnp.sin(ang)
    m = jnp.arange(LANES) % DIFF_HD
    cos_l = cos[:, m % half]
    sin_l = sin[:, m % half]
    c = jnp.where(m < ROT_DIMS, cos_l, 1.0)
    s_lo = jnp.where(m < half, -sin_l, 0.0)
    s_hi = jnp.where((m >= half) & (m < ROT_DIMS), sin_l, 0.0)
    return c, s_lo, s_hi


def _qkv_kernel(x_ref, g_ref, w_ref, c_ref, slo_ref, shi_ref, qt_ref, k_ref, vt_ref):
    h = _rms(x_ref[...], g_ref[...]).astype(BF16)
    qkv = _dot(h, w_ref[...])
    c = c_ref[...]
    s_lo = slo_ref[...]
    s_hi = shi_ref[...]
    half = ROT_DIMS // 2

    def rope(t):
        return t * c + pltpu.roll(t, LANES - half, 1) * s_lo + pltpu.roll(t, half, 1) * s_hi

    for j in range(D_MODEL // LANES):
        ls = slice(j * LANES, (j + 1) * LANES)
        q = rope(qkv[:, ls]) * (DIFF_HD ** -0.5 * LOG2E)
        qt_ref[ls, :] = q.T.astype(BF16)
        k_ref[j, 0] = rope(qkv[:, D_MODEL + j * LANES:D_MODEL + (j + 1) * LANES]).astype(BF16)
        vt_ref[j, 0] = qkv[:, 2 * D_MODEL + j * LANES:2 * D_MODEL + (j + 1) * LANES].T.astype(BF16)


def _qkv_layer(x, g, w_qkv, c, s_lo, s_hi, *, tm):
    L = x.shape[0]
    n = L // tm
    row = pl.BlockSpec((tm, D_MODEL), lambda i: (i, 0))
    tab = pl.BlockSpec((tm, LANES), lambda i: (i, 0))
    return pl.pallas_call(
        _qkv_kernel,
        grid=(n,),
        in_specs=[row, pl.BlockSpec((1, D_MODEL), lambda i: (0, 0)),
                  _resident((D_MODEL, 3 * D_MODEL)), tab, tab, tab],
        out_specs=[pl.BlockSpec((D_MODEL, tm), lambda i: (0, i)),
                   pl.BlockSpec((DIFF_HEADS, 1, tm, LANES), lambda i: (0, i, 0, 0)),
                   pl.BlockSpec((DIFF_HEADS, 1, LANES, tm), lambda i: (0, i, 0, 0))],
        out_shape=[jax.ShapeDtypeStruct((D_MODEL, L), BF16),
                   jax.ShapeDtypeStruct((DIFF_HEADS, n, tm, LANES), BF16),
                   jax.ShapeDtypeStruct((DIFF_HEADS, n, LANES, tm), BF16)],
        compiler_params=_params("parallel"),
        name="diff_qkv",
    )(x, g, w_qkv, c, s_lo, s_hi)


def _attn_kernel(qt_ref, k_ref, vt_ref, lamv_ref, sg_ref, bias_ref, o_ref, m_scr, l_scr, acc_scr,
                 sa_scr, sb_scr, ma_scr, mb_scr, *, lambda_init):
    qi = pl.program_id(1)
    i = qi // 2
    qt = qt_ref[...]
    row = lax.broadcasted_iota(jnp.int32, qt.shape, 0)
    zero = jnp.zeros_like(qt)
    q_half = (jnp.where(row < DIFF_HD, qt, zero), jnp.where(row >= DIFF_HD, qt, zero))

    m_scr[...] = jnp.full_like(m_scr, NEG_INF)
    l_scr[...] = jnp.zeros_like(l_scr)
    acc_scr[...] = jnp.zeros_like(acc_scr)

    tq = qt.shape[1]

    def scores(j, slot, part=None, masked=False):
        s_ref, smax_ref = slot
        rows = slice(None) if part is None else slice(part * tq, (part + 1) * tq)
        k = k_ref[0, j, rows, :]
        for hf in range(2):
            s = _dot(k, q_half[hf])
            if masked:
                s = s + bias_ref[...]
            if part is None:
                s_ref[hf] = s
            else:
                s_ref[hf, :tq, :] = s
            smax_ref[hf] = jnp.max(s, axis=0, keepdims=True)

    def update(j, slot, part=None):
        s_ref, smax_ref = slot
        cols = slice(None) if part is None else slice(part * tq, (part + 1) * tq)
        vt = vt_ref[0, j, :, cols]
        for hf in range(2):
            m_prev = m_scr[hf]
            m_new = jnp.maximum(m_prev, smax_ref[hf])
            alpha = jnp.exp2(m_prev - m_new)
            s = s_ref[hf] if part is None else s_ref[hf, :tq, :]
            p = jnp.exp2(s - m_new)
            l_scr[hf] = alpha * l_scr[hf] + jnp.sum(p, axis=0, keepdims=True)
            acc_scr[hf] = alpha * acc_scr[hf] + _dot(vt, p.astype(BF16))
            m_scr[hf] = m_new

    slot_a, slot_b = (sa_scr, ma_scr), (sb_scr, mb_scr)

    def pair(j):
        scores(j + 1, slot_b)
        update(j, slot_a)
        scores(j + 2, slot_a)
        update(j + 1, slot_b)

    unroll = 4

    def trip(t, carry):
        for u in range(unroll):
            pair(2 * (unroll * t + u))
        return carry

    def diagonal(cond, prefix, pending, free, jp):
        @pl.when(cond & (qi % 2 == 0))
        def _():
            prefix()
            scores(i, free, part=0, masked=True)
            if jp is not None:
                update(jp, pending)
            update(i, free, part=0)

        @pl.when(cond & (qi % 2 == 1))
        def _():
            prefix()
            scores(i, free, part=0)
            if jp is not None:
                update(jp, pending)
            scores(i, pending, part=1, masked=True)
            update(i, free, part=0)
            update(i, pending, part=1)

    diagonal(i == 0, lambda: None, slot_b, slot_a, None)

    @pl.when(i > 0)
    def _():
        scores(0, slot_a)

    n_pairs = jnp.maximum(i - 1, 0) // 2
    lax.fori_loop(0, n_pairs // unroll, trip, 0)
    done = (n_pairs // unroll) * unroll
    for width in (2, 1):
        @pl.when((n_pairs // width) % 2 == 1)
        def _(width=width, done=done):
            for u in range(width):
                pair(2 * (done + u))
        done = done + jnp.where((n_pairs // width) % 2 == 1, width, 0)

    diagonal(i % 2 == 1, lambda: None, slot_a, slot_b, i - 1)

    def two_left():
        scores(i - 1, slot_b)
        update(i - 2, slot_a)

    diagonal((i % 2 == 0) & (i > 0), two_left, slot_b, slot_a, i - 1)

    lv = lamv_ref[...]
    lam = (jnp.exp(jnp.sum(lv[0:1] * lv[1:2], axis=-1, keepdims=True))
           - jnp.exp(jnp.sum(lv[2:3] * lv[3:4], axis=-1, keepdims=True)) + lambda_init)
    o = acc_scr[0] / l_scr[0] - lam * (acc_scr[1] / l_scr[1])
    o = o * lax.rsqrt(jnp.mean(o * o, axis=0, keepdims=True) + SUBLN_EPS) * sg_ref[...]
    o_ref[...] = (o * (1.0 - lambda_init)).astype(BF16)


def _attn_layer(qt, k, vt, lamv, subln_g, *, tq, lambda_init):
    n, tk = k.shape[1], k.shape[2]
    L = n * tk
    assert tk == 2 * tq, (tk, tq)
    stat = pltpu.VMEM((2, 1, tq), F32)
    chunk = np.arange(tq) // CHUNK
    bias = np.where(chunk[:, None] <= chunk[None, :], 0.0, NEG_INF)
    return pl.pallas_call(
        functools.partial(_attn_kernel, lambda_init=lambda_init),
        grid=(DIFF_HEADS, L // tq),
        in_specs=[
            pl.BlockSpec((LANES, tq), lambda p, i: (p, i)),
            pl.BlockSpec((1, n, tk, LANES), lambda p, i: (p, 0, 0, 0)),
            pl.BlockSpec((1, n, LANES, tk), lambda p, i: (p, 0, 0, 0)),
            pl.BlockSpec((4, DIFF_HD), lambda p, i: (0, 0)),
            pl.BlockSpec((LANES, 1), lambda p, i: (0, 0)),
            pl.BlockSpec((tq, tq), lambda p, i: (0, 0), pipeline_mode=pl.Buffered(1)),
        ],
        out_specs=pl.BlockSpec((LANES, tq), lambda p, i: (p, i)),
        out_shape=jax.ShapeDtypeStruct((D_MODEL, L), BF16),
        scratch_shapes=[stat, stat, pltpu.VMEM((2, LANES, tq), F32),
                        pltpu.VMEM((2, tk, tq), F32), pltpu.VMEM((2, tk, tq), F32), stat, stat],
        compiler_params=_params("parallel", "arbitrary"),
        name="diff_attn",
    )(qt, k, vt, lamv, subln_g, jnp.asarray(bias, F32))


def _diff_mixer_layer(x, g, positions, w_qkv, lam_q1, lam_k1, lam_q2, lam_k2, subln_g, w_o,
                      lambda_init, *, tq, tk):
    c, s_lo, s_hi = _rope_tables(positions)
    qt, k4, vt4 = _qkv_layer(x, g, w_qkv.astype(BF16), c, s_lo, s_hi, tm=tk)
    lamv = jnp.stack([lam_q1, lam_k1, lam_q2, lam_k2]).astype(F32)
    at = _attn_layer(qt, k4, vt4, lamv, subln_g.reshape(LANES, 1).astype(F32), tq=tq,
                     lambda_init=lambda_init)
    return at, w_o.astype(BF16)


def _trunk(x, positions, norm_mix, norm_ffn, norm_final,
           s5_lam_re, s5_lam_im, s5_log_dt, s5_b_re, s5_b_im, s5_c_re, s5_c_im, s5_d, s5_w_glu,
           gla_w_in, gla_w_a2, gla_b_a, gla_norm, gla_w_o,
           diff_w_qkv, diff_lam_q1, diff_lam_k1, diff_lam_q2, diff_lam_k2, diff_subln, diff_w_o,
           ffn_w_gate_up, ffn_w_down, *, tm, tm_core, tm_gla, tq, tk, tm_ffn, th):
    depth = norm_mix.shape[0]
    gfinal = norm_final.reshape(1, D_MODEL)
    gla_w_a1 = gla_w_in[:, :, 2 * GLA_DK + 2 * GLA_DV:]
    gla_w_in = _to_bf16(gla_w_in, cols=2 * GLA_DK + 2 * GLA_DV)
    s5_w_glu, gla_w_o, diff_w_qkv, diff_w_o, ffn_w_gate_up, ffn_w_down = (
        _to_bf16(w) for w in (s5_w_glu, gla_w_o, diff_w_qkv, diff_w_o, ffn_w_gate_up, ffn_w_down))
    for layer in range(depth):
        kind = layer % N_MIXERS
        idx = layer // N_MIXERS
        g = norm_mix[layer].reshape(1, D_MODEL)
        if kind == 0:
            pre, tm_f = "s5", tm
            pre_args = _s5_mixer_layer(x, g, s5_lam_re[idx], s5_lam_im[idx], s5_log_dt[idx],
                                       s5_b_re[idx], s5_b_im[idx], s5_c_re[idx], s5_c_im[idx],
                                       s5_d[idx], s5_w_glu[idx], tm=tm, tm_core=tm_core)
        elif kind == 1:
            pre, tm_f, pre_args = None, tm_ffn, ()
            x = _gla_mixer_layer(x, g, gla_w_in[idx], gla_w_a1[idx], gla_w_a2[idx], gla_b_a[idx],
                                 gla_norm[idx], gla_w_o[idx], tm=tm_gla)
        else:
            pre, tm_f = "proj", tm
            lambda_init = 0.8 - 0.6 * math.exp(-0.3 * layer)
            pre_args = _diff_mixer_layer(x, g, positions, diff_w_qkv[idx], diff_lam_q1[idx],
                                         diff_lam_k1[idx], diff_lam_q2[idx], diff_lam_k2[idx],
                                         diff_subln[idx], diff_w_o[idx], lambda_init, tq=tq, tk=tk)
        x = _ffn_layer(x, pre_args, norm_ffn[layer].reshape(1, D_MODEL), ffn_w_gate_up, ffn_w_down,
                       gfinal, layer=layer, final=(layer == depth - 1), tm=tm_f, th=th, pre=pre)
    return x


def kernel(x, positions, norm_mix, norm_ffn, norm_final, s5_lam_re, s5_lam_im, s5_log_dt, s5_b_re, s5_b_im, s5_c_re, s5_c_im, s5_d, s5_w_glu, gla_w_in, gla_w_a2, gla_b_a, gla_norm, gla_w_o, diff_w_qkv, diff_lam_q1, diff_lam_k1, diff_lam_q2, diff_lam_k2, diff_subln, diff_w_o, ffn_w_gate_up, ffn_w_down):
    bsz, seq, _ = x.shape
    outs = []
    for b in range(bsz):
        outs.append(_trunk(
            x[b], positions[b], norm_mix, norm_ffn, norm_final,
            s5_lam_re, s5_lam_im, s5_log_dt, s5_b_re, s5_b_im, s5_c_re, s5_c_im, s5_d, s5_w_glu,
            gla_w_in, gla_w_a2, gla_b_a, gla_norm, gla_w_o,
            diff_w_qkv, diff_lam_q1, diff_lam_k1, diff_lam_q2, diff_lam_k2, diff_subln, diff_w_o,
            ffn_w_gate_up, ffn_w_down,
            tm=512, tm_core=8192, tm_gla=512, tq=512, tk=1024, tm_ffn=1024, th=256))
    return jnp.stack(outs)
```
